```python
import math
import jax
import jax.numpy as jnp
from jax import lax
import numpy as np

D_MODEL = 2048
BATCH = 4
SEQ = 2048
DEPTH = 2
DEC_BATCH = 32
DEC_SEQ = 1
PAST_LEN = 8192
PAGE_SIZE = 128

D_MIX = D_MODEL
N_MIXERS = 4
W_GROUP = D_MIX // N_MIXERS
H_SB = 8
HD_SB = W_GROUP // H_SB
H_DIFF = 4
HD_DIFF = W_GROUP // (2 * H_DIFF)
W_LRU = W_GROUP
N_LRU_BLOCKS = 8
LRU_BLOCK = W_LRU // N_LRU_BLOCKS
CONV_W = 4
LRU_C = 8.0
H_ML = 4
HD_ML = W_GROUP // H_ML
PROJ_SPLITS = (W_GROUP,) * 12 + (H_ML, H_ML)
PROJ_DIM = 12 * W_GROUP + 2 * H_ML
D_FF = ((8 * D_MODEL) // 3 + 255) // 256 * 256
N_EXPERTS = 8
TOP_K = 2
D_FF_EXPERT = D_FF // TOP_K
Q_BLOCK = 128
ML_CHUNK = 128
EPS = 1e-6

kernel_name = 'hybrid_sb_diff_rglru_mlstm_decoder_step'


def rmsnorm(x, g):
    xf = x.astype(jnp.float32)
    y = xf * lax.rsqrt(jnp.mean(xf * xf, axis=-1, keepdims=True) + EPS)
    return (y * g.astype(jnp.float32)).astype(x.dtype)


def _ada(c, w, b):
    m = jax.nn.silu(c) @ w + b
    return jnp.split(m[:, None, :], 6, axis=-1)


def _split_proj(z):
    outs, off = [], 0
    for n in PROJ_SPLITS:
        outs.append(z[..., off:off + n])
        off += n
    return outs


def _query_blocks(T):
    qb = Q_BLOCK if T % Q_BLOCK == 0 else T
    return T // qb, qb


def stick_breaking_attention(q, k_past, v_past, k_new, v_new, q_pos, k_pos):
    B, T, H, d = q.shape
    P = k_past.shape[1]
    nb, qb = _query_blocks(T)
    q_blocks = q.reshape(B, nb, qb, H, d).swapaxes(0, 1)
    pos_blocks = q_pos.reshape(nb, qb)
    scale = 1.0 / math.sqrt(d)

    def block(args):
        qblk, pblk = args
        z = jnp.concatenate([
            jnp.einsum('bqhd,bshd->bhqs', qblk, k_past, preferred_element_type=jnp.float32),
            jnp.einsum('bqhd,bshd->bhqs', qblk, k_new, preferred_element_type=jnp.float32)],
            axis=-1) * scale
        mask = k_pos[None, :] < pblk[:, None]
        log_keep = jnp.where(mask, jax.nn.log_sigmoid(-z), 0.0)
        log_after = lax.cumsum(log_keep, axis=3, reverse=True) - log_keep
        w = jnp.where(mask, jnp.exp(jax.nn.log_sigmoid(z) + log_after), 0.0).astype(v_new.dtype)
        return (jnp.einsum('bhqs,bshd->bqhd', w[..., :P], v_past)
                + jnp.einsum('bhqs,bshd->bqhd', w[..., P:], v_new))

    out = lax.map(block, (q_blocks, pos_blocks))
    return out.swapaxes(0, 1).reshape(B, T, H, d)


def differential_attention(q, k_past, v_past, k_new, v_new, q_pos, k_pos, lam, slopes):
    B, T, H, _, d = q.shape
    E = v_new.shape[-1]
    P = k_past.shape[1]
    nb, qb = _query_blocks(T)
    q_blocks = q.reshape(B, nb, qb, H, 2, d).swapaxes(0, 1)
    pos_blocks = q_pos.reshape(nb, qb)
    scale = 1.0 / math.sqrt(d)

    def block(args):
        qblk, pblk = args
        s = jnp.concatenate([
            jnp.einsum('bqhcd,bshcd->bhcqs', qblk, k_past, preferred_element_type=jnp.float32),
            jnp.einsum('bqhcd,bshcd->bhcqs', qblk, k_new, preferred_element_type=jnp.float32)],
            axis=-1) * scale
        dist = (pblk[:, None] - k_pos[None, :]).astype(jnp.float32)
        s = s - slopes[None, :, None, None, None] * dist
        s = jnp.where(k_pos[None, :] <= pblk[:, None], s, -jnp.inf)
        p = jax.nn.softmax(s, axis=-1)
        w = (p[:, :, 0] - lam * p[:, :, 1]).astype(v_new.dtype)
        return (jnp.einsum('bhqs,bshe->bqhe', w[..., :P], v_past)
                + jnp.einsum('bhqs,bshe->bqhe', w[..., P:], v_new))

    out = lax.map(block, (q_blocks, pos_blocks))
    return out.swapaxes(0, 1).reshape(B, T, H, E)


def _lin_combine(e1, e2):
    a1, b1 = e1
    a2, b2 = e2
    return a1 * a2, a2 * b1 + b2


def rg_lru(x_in, conv_buf, h0, conv_w, conv_b, w_r, b_r, w_i, b_i, lam):
    B, T, W = x_in.shape
    f32 = jnp.float32
    xs = jnp.concatenate([conv_buf.astype(x_in.dtype), x_in], axis=1)
    xc = conv_b
    for j in range(CONV_W):
        xc = xc + xs[:, j:j + T] * conv_w[j]
    xg = xc.reshape(B, T, N_LRU_BLOCKS, LRU_BLOCK)
    r = jax.nn.sigmoid((jnp.einsum('btnk,nkj->btnj', xg, w_r).reshape(B, T, W) + b_r).astype(f32))
    i = jax.nn.sigmoid((jnp.einsum('btnk,nkj->btnj', xg, w_i).reshape(B, T, W) + b_i).astype(f32))
    log_a = -LRU_C * r * jax.nn.softplus(-lam.astype(f32))
    a = jnp.exp(log_a)
    u = jnp.sqrt(-jnp.expm1(2.0 * log_a)) * (i * xc.astype(f32))
    u = u.at[:, 0].add(a[:, 0] * h0.astype(f32))
    _, h = lax.associative_scan(_lin_combine, (a, u), axis=1)
    return h.astype(x_in.dtype), xs[:, T:], h[:, -1].astype(x_in.dtype)


def mlstm_chunkwise(q, k, v, i_pre, f_pre, c0, n0, m0):
    B, T, H, d = q.shape
    L = ML_CHUNK if T % ML_CHUNK == 0 else T
    nc = T // L
    f32 = jnp.float32

    def chunks(t):
        t = t.astype(f32).reshape((B, nc, L, H) + t.shape[3:])
        return jnp.moveaxis(jnp.moveaxis(t, 1, 0), 2, 3)

    qs, ks, vs = chunks(q), chunks(k) / math.sqrt(d), chunks(v)
    i_s, lf_s = chunks(i_pre), jax.nn.log_sigmoid(chunks(f_pre))
    causal = jnp.tril(jnp.ones((L, L), dtype=bool))

    def step(carry, inp):
        c, n, m = carry
        qc, kc, vc, ic, lfc = inp
        b = jnp.cumsum(lfc, axis=-1)
        dmat = jnp.where(causal, b[..., :, None] - b[..., None, :] + ic[..., None, :], -jnp.inf)
        m_t = jnp.maximum(b + m[..., None], jnp.max(dmat, axis=-1))
        w = jnp.exp(dmat - m_t[..., None])
        inter = jnp.exp(b + m[..., None] - m_t)
        sw = w * jnp.einsum('bhtd,bhsd->bhts', qc, kc)
        num = (jnp.einsum('bhts,bhse->bhte', sw, vc)
               + inter[..., None] * jnp.einsum('bhed,bhtd->bhte', c, qc))
        den = jnp.sum(sw, axis=-1) + inter * jnp.einsum('bhd,bhtd->bht', n, qc)
        h = num / jnp.maximum(jnp.abs(den), jnp.exp(-m_t))[..., None]
        m_new = m_t[..., -1]
        decay = inter[..., -1]
        w_last = w[..., -1, :]
        c_new = decay[..., None, None] * c + jnp.einsum('bhs,bhse,bhsd->bhed', w_last, vc, kc)
        n_new = decay[..., None] * n + jnp.einsum('bhs,bhsd->bhd', w_last, kc)
        return (c_new, n_new, m_new), h

    (c, n, m), hs = lax.scan(step, (c0.astype(f32), n0.astype(f32), m0.astype(f32)),
                             (qs, ks, vs, i_s, lf_s))
    hs = jnp.moveaxis(hs, 0, 1).swapaxes(2, 3).reshape(B, T, H, d)
    dt = q.dtype
    return hs.astype(dt), c.astype(dt), n.astype(dt), m.astype(dt)


def _swiglu(h, w_gate, w_up, w_down):
    return (jax.nn.silu(h @ w_gate) * (h @ w_up)) @ w_down


def _moe_swiglu(h, w_router, b_router, w_gate, w_up, w_down):
    logits = jnp.einsum('btd,de->bte', h, w_router).astype(jnp.float32) + b_router.astype(jnp.float32)
    top_logits, top_idx = lax.top_k(logits, TOP_K)
    top_w = jax.nn.softmax(top_logits, axis=-1)
    combine = jnp.sum(jax.nn.one_hot(top_idx, N_EXPERTS, dtype=jnp.float32) * top_w[..., None],
                      axis=-2).astype(h.dtype)
    out = jnp.zeros_like(h)
    for e in range(N_EXPERTS):
        out = out + combine[..., e:e + 1] * _swiglu(h, w_gate[e], w_up[e], w_down[e])
    return out


def _token_mixers(h, l, past, st, p):
    B, T, _ = h.shape
    sb_kp, sb_vp, df_kp, df_vp = past
    P = sb_kp.shape[1]
    q_pos = P + jnp.arange(T, dtype=jnp.int32)
    k_pos = jnp.arange(P + T, dtype=jnp.int32)
    z = jnp.einsum('btd,dk->btk', h, p['w_in'][l])
    (sb_q, sb_k, sb_v, df_q, df_k, df_v, lru_x, lru_g,
     ml_q, ml_k, ml_v, ml_o, ml_i, ml_f) = _split_proj(z)
    sb_q = sb_q.reshape(B, T, H_SB, HD_SB)
    sb_k = sb_k.reshape(B, T, H_SB, HD_SB)
    sb_v = sb_v.reshape(B, T, H_SB, HD_SB)
    y_sb = stick_breaking_attention(sb_q, sb_kp, sb_vp, sb_k, sb_v, q_pos, k_pos)
    y_sb = rmsnorm(y_sb, p['g_sb_out'][l]).reshape(B, T, W_GROUP)
    df_q = df_q.reshape(B, T, H_DIFF, 2, HD_DIFF)
    df_k = df_k.reshape(B, T, H_DIFF, 2, HD_DIFF)
    df_v = df_v.reshape(B, T, H_DIFF, 2 * HD_DIFF)
    lam_init = 0.8 - 0.6 * math.exp(-0.3 * l)
    lv = p['diff_lam'][l].astype(jnp.float32)
    lam = jnp.exp(jnp.sum(lv[0] * lv[1])) - jnp.exp(jnp.sum(lv[2] * lv[3])) + lam_init
    slopes = jnp.exp2(-8.0 * jnp.arange(1, H_DIFF + 1, dtype=jnp.float32) / H_DIFF)
    y_df = differential_attention(df_q, df_kp, df_vp, df_k, df_v, q_pos, k_pos, lam, slopes)
    y_df = (rmsnorm(y_df, p['g_diff_out'][l]) * (1.0 - lam_init)).reshape(B, T, W_GROUP)
    conv0, h0, c0, n0, m0 = st
    y_lru, conv_new, h_new = rg_lru(lru_x, conv0, h0, p['lru_conv_w'][l], p['lru_conv_b'][l],
                                    p['lru_w_r'][l], p['lru_b_r'][l], p['lru_w_i'][l],
                                    p['lru_b_i'][l], p['lru_lam'][l])
    y_lru = y_lru * jax.nn.gelu(lru_g)
    y_lru = rmsnorm(y_lru.reshape(B, T, N_LRU_BLOCKS, LRU_BLOCK), p['g_lru_out'][l]).reshape(B, T, W_GROUP)
    ml_q = ml_q.reshape(B, T, H_ML, HD_ML)
    ml_k = ml_k.reshape(B, T, H_ML, HD_ML)
    ml_v = ml_v.reshape(B, T, H_ML, HD_ML)
    y_ml, c_new, n_new, m_new = mlstm_chunkwise(ml_q, ml_k, ml_v, ml_i + p['ml_b_i'][l],
                                                ml_f + p['ml_b_f'][l], c0, n0, m0)
    y_ml = y_ml.reshape(B, T, W_GROUP) * jax.nn.sigmoid(ml_o)
    y_ml = rmsnorm(y_ml.reshape(B, T, H_ML, HD_ML), p['g_ml_out'][l]).reshape(B, T, W_GROUP)
    y = jnp.concatenate([y_sb, y_df, y_lru, y_ml], axis=-1) @ p['w_out'][l]
    return y, (sb_k, sb_v, df_k, df_v, conv_new, h_new, c_new, n_new, m_new)


def _run_trunk(x, c, get_past, get_state, p):
    news = []
    for l in range(DEPTH):
        sh1, sc1, g1, sh2, sc2, g2 = _ada(c, p['w_ada'][l], p['b_ada'][l])
        h = rmsnorm(x, p['g_norm1'][l]) * (1.0 + sc1) + sh1
        y, new = _token_mixers(h, l, get_past(l), get_state(l), p)
        x = x + g1 * y
        h = rmsnorm(x, p['g_norm2'][l]) * (1.0 + sc2) + sh2
        j = l // 2
        if l % 2 == 0:
            f = _swiglu(h, p['ffn_w_gate'][j], p['ffn_w_up'][j], p['ffn_w_down'][j])
        else:
            f = _moe_swiglu(h, p['moe_w_router'][j], p['moe_b_router'][j], p['moe_w_gate'][j],
                            p['moe_w_up'][j], p['moe_w_down'][j])
        x = x + g2 * f
        news.append(new)
    y = rmsnorm(x, p['g_final'])
    stacked = tuple(jnp.stack([nw[i] for nw in news], axis=0) for i in range(9))
    return y, stacked


def _gather_pages(pool, l, page_table):
    g = pool[l, page_table]
    return g.reshape((g.shape[0], g.shape[1] * g.shape[2]) + g.shape[3:])


def setup_inputs(seed: int = 0) -> dict:
    key = jax.random.key(seed)
    keys = jax.random.split(key, 48)

    def nrm(i, shape, s=1.0):
        return s * jax.random.normal(keys[i], shape, jnp.float32)

    def gain(i, shape):
        return 1.0 + nrm(i, shape, 0.02)

    n_pages = PAST_LEN // PAGE_SIZE
    n_used = DEC_BATCH * n_pages
    n_phys = n_used + max(1, n_used // 4)
    n_dense = (DEPTH + 1) // 2
    n_moe = DEPTH // 2
    page_table = jax.random.permutation(keys[11], n_phys)[:n_used].reshape(DEC_BATCH, n_pages).astype(jnp.int32)
    u = jax.random.uniform(keys[29], (DEPTH, W_LRU), jnp.float32, 0.9, 0.999)
    a = u ** (1.0 / LRU_C)
    lru_lam = jnp.log(a) - jnp.log1p(-a)
    ml_b_f = jnp.linspace(3.0, 6.0, H_ML, dtype=jnp.float32)[None, :] + nrm(32, (DEPTH, H_ML), 0.1)
    sd = D_MODEL ** -0.5
    return {
        'x_prompt': nrm(0, (BATCH, SEQ, D_MODEL)),
        'x_sample': nrm(1, (DEC_BATCH, DEC_SEQ, D_MODEL)),
        'cache_sb_k': nrm(2, (DEPTH, n_phys, PAGE_SIZE, H_SB, HD_SB)),
        'cache_sb_v': nrm(3, (DEPTH, n_phys, PAGE_SIZE, H_SB, HD_SB)),
        'cache_diff_k': nrm(4, (DEPTH, n_phys, PAGE_SIZE, H_DIFF, 2, HD_DIFF)),
        'cache_diff_v': nrm(5, (DEPTH, n_phys, PAGE_SIZE, H_DIFF, 2 * HD_DIFF)),
        'state_lru_conv': nrm(6, (DEPTH, DEC_BATCH, CONV_W - 1, W_LRU)),
        'state_lru_h': nrm(7, (DEPTH, DEC_BATCH, W_LRU), 0.5),
        'state_mlstm_c': nrm(8, (DEPTH, DEC_BATCH, H_ML, HD_ML, HD_ML), 0.3),
        'state_mlstm_n': nrm(9, (DEPTH, DEC_BATCH, H_ML, HD_ML)),
        'state_mlstm_m': nrm(10, (DEPTH, DEC_BATCH, H_ML)),
        'page_table': page_table,
        'c_prompt': nrm(12, (BATCH, D_MODEL)),
        'c_sample': nrm(13, (DEC_BATCH, D_MODEL)),
        'w_ada': nrm(14, (DEPTH, D_MODEL, 6 * D_MODEL), 0.5 * sd),
        'b_ada': nrm(15, (DEPTH, 6 * D_MODEL), 0.02),
        'g_norm1': gain(16, (DEPTH, D_MODEL)),
        'g_norm2': gain(17, (DEPTH, D_MODEL)),
        'w_in': nrm(18, (DEPTH, D_MODEL, PROJ_DIM), sd),
        'w_out': nrm(19, (DEPTH, D_MIX, D_MODEL), D_MIX ** -0.5),
        'g_sb_out': gain(20, (DEPTH, H_SB, HD_SB)),
        'diff_lam': nrm(21, (DEPTH, 4, HD_DIFF), 0.1),
        'g_diff_out': gain(22, (DEPTH, H_DIFF, 2 * HD_DIFF)),
        'lru_conv_w': nrm(23, (DEPTH, CONV_W, W_LRU), CONV_W ** -0.5),
        'lru_conv_b': nrm(24, (DEPTH, W_LRU), 0.02),
        'lru_w_r': nrm(25, (DEPTH, N_LRU_BLOCKS, LRU_BLOCK, LRU_BLOCK), LRU_BLOCK ** -0.5),
        'lru_b_r': nrm(26, (DEPTH, W_LRU), 0.02),
        'lru_w_i': nrm(27, (DEPTH, N_LRU_BLOCKS, LRU_BLOCK, LRU_BLOCK), LRU_BLOCK ** -0.5),
        'lru_b_i': nrm(28, (DEPTH, W_LRU), 0.02),
        'lru_lam': lru_lam,
        'g_lru_out': gain(30, (DEPTH, N_LRU_BLOCKS, LRU_BLOCK)),
        'ml_b_i': nrm(31, (DEPTH, H_ML), 0.1),
        'ml_b_f': ml_b_f,
        'g_ml_out': gain(33, (DEPTH, H_ML, HD_ML)),
        'ffn_w_gate': nrm(34, (n_dense, D_MODEL, D_FF), sd),
        'ffn_w_up': nrm(35, (n_dense, D_MODEL, D_FF), sd),
        'ffn_w_down': nrm(36, (n_dense, D_FF, D_MODEL), D_FF ** -0.5),
        'moe_w_router': nrm(37, (n_moe, D_MODEL, N_EXPERTS), sd),
        'moe_b_router': nrm(38, (n_moe, N_EXPERTS), 0.01),
        'moe_w_gate': nrm(39, (n_moe, N_EXPERTS, D_MODEL, D_FF_EXPERT), sd),
        'moe_w_up': nrm(40, (n_moe, N_EXPERTS, D_MODEL, D_FF_EXPERT), sd),
        'moe_w_down': nrm(41, (n_moe, N_EXPERTS, D_FF_EXPERT, D_MODEL), D_FF_EXPERT ** -0.5),
        'g_final': gain(42, (D_MODEL,)),
    }


def reference(x_prompt, x_sample, cache_sb_k, cache_sb_v, cache_diff_k, cache_diff_v,
              state_lru_conv, state_lru_h, state_mlstm_c, state_mlstm_n, state_mlstm_m,
              page_table, c_prompt, c_sample, w_ada, b_ada, g_norm1, g_norm2, w_in, w_out,
              g_sb_out, diff_lam, g_diff_out, lru_conv_w, lru_conv_b, lru_w_r, lru_b_r,
              lru_w_i, lru_b_i, lru_lam, g_lru_out, ml_b_i, ml_b_f, g_ml_out,
              ffn_w_gate, ffn_w_up, ffn_w_down, moe_w_router, moe_b_router,
              moe_w_gate, moe_w_up, moe_w_down, g_final):
    p = dict(w_ada=w_ada, b_ada=b_ada, g_norm1=g_norm1, g_norm2=g_norm2, w_in=w_in, w_out=w_out,
             g_sb_out=g_sb_out, diff_lam=diff_lam, g_diff_out=g_diff_out, lru_conv_w=lru_conv_w,
             lru_conv_b=lru_conv_b, lru_w_r=lru_w_r, lru_b_r=lru_b_r, lru_w_i=lru_w_i,
             lru_b_i=lru_b_i, lru_lam=lru_lam, g_lru_out=g_lru_out, ml_b_i=ml_b_i, ml_b_f=ml_b_f,
             g_ml_out=g_ml_out, ffn_w_gate=ffn_w_gate, ffn_w_up=ffn_w_up, ffn_w_down=ffn_w_down,
             moe_w_router=moe_w_router, moe_b_router=moe_b_router, moe_w_gate=moe_w_gate,
             moe_w_up=moe_w_up, moe_w_down=moe_w_down, g_final=g_final)
    Bp = x_prompt.shape[0]
    dt = x_prompt.dtype

    def prompt_past(l):
        return (jnp.zeros((Bp, 0, H_SB, HD_SB), dt), jnp.zeros((Bp, 0, H_SB, HD_SB), dt),
                jnp.zeros((Bp, 0, H_DIFF, 2, HD_DIFF), dt), jnp.zeros((Bp, 0, H_DIFF, 2 * HD_DIFF), dt))

    def prompt_state(l):
        return (jnp.zeros((Bp, CONV_W - 1, W_LRU), dt), jnp.zeros((Bp, W_LRU), dt),
                jnp.zeros((Bp, H_ML, HD_ML, HD_ML), dt), jnp.zeros((Bp, H_ML, HD_ML), dt),
                jnp.zeros((Bp, H_ML), dt))

    def sample_past(l):
        return (_gather_pages(cache_sb_k, l, page_table), _gather_pages(cache_sb_v, l, page_table),
                _gather_pages(cache_diff_k, l, page_table), _gather_pages(cache_diff_v, l, page_table))

    def sample_state(l):
        return (state_lru_conv[l], state_lru_h[l], state_mlstm_c[l], state_mlstm_n[l], state_mlstm_m[l])

    y_prompt, (p_sb_k, p_sb_v, p_df_k, p_df_v, p_conv, p_h, p_c, p_n, p_m) = _run_trunk(
        x_prompt, c_prompt, prompt_past, prompt_state, p)
    y_sample, (s_sb_k, s_sb_v, s_df_k, s_df_v, s_conv, s_h, s_c, s_n, s_m) = _run_trunk(
        x_sample, c_sample, sample_past, sample_state, p)
    return (y_prompt, y_sample,
            p_sb_k, p_sb_v, p_df_k, p_df_v, p_conv, p_h, p_c, p_n, p_m,
            s_sb_k, s_sb_v, s_df_k, s_df_v, s_conv, s_h, s_c, s_n, s_m)
```

```python
import functools
import math

import jax
import jax.numpy as jnp
from jax import lax
from jax.experimental import pallas as pl
from jax.experimental.pallas import tpu as pltpu

F32 = jnp.float32
BF16 = jnp.bfloat16
HI = lax.Precision.HIGHEST
EPS = 1e-6
LRU_C = 8.0
LANE = 128
VMEM_LIMIT = 56 * 1024 * 1024

NT_DIMS = (((1,), (1,)), ((), ()))


def _cp(*sem):
    return pltpu.CompilerParams(dimension_semantics=sem, vmem_limit_bytes=VMEM_LIMIT)


def _dot(a, b, precise):
    if precise:
        return jnp.dot(a, b, precision=HI, preferred_element_type=F32)
    return jnp.dot(a.astype(BF16), b.astype(BF16), preferred_element_type=F32)


def _softplus(x):
    return jnp.maximum(x, 0.0) + jnp.log1p(jnp.exp(-jnp.abs(x)))


def _sigmoid(x):
    return 1.0 / (1.0 + jnp.exp(-x))


def _group_mean_matrix(width, group):
    r = lax.broadcasted_iota(jnp.int32, (width, width), 0) // group
    c = lax.broadcasted_iota(jnp.int32, (width, width), 1) // group
    return jnp.where(r == c, 1.0 / group, 0.0).astype(F32)


def _pick_tile(n, pref):
    t = min(n, pref)
    while n % t:
        t //= 2
    return t


def _mm_kernel(*refs, nk, precise, has_bias, silu_a, resid, b_transposed):
    a_ref, b_ref = refs[0], refs[1]
    i = 2
    bias_ref = x_ref = g_ref = None
    if has_bias:
        bias_ref = refs[i]
        i += 1
    if resid:
        x_ref, g_ref = refs[i], refs[i + 1]
        i += 2
    o_ref, acc_ref = refs[i], refs[i + 1]
    k = pl.program_id(2)
    a = a_ref[...]
    if silu_a:
        a = a * _sigmoid(a)
    if b_transposed:
        assert precise
        part = lax.dot_general(a, b_ref[...], NT_DIMS, precision=HI, preferred_element_type=F32)
    else:
        part = _dot(a, b_ref[...], precise)

    def finish(acc):
        if has_bias:
            acc = acc + bias_ref[...]
        if resid:
            g = g_ref[...].reshape(-1, acc.shape[-1])
            acc = x_ref[...] + g * acc
        o_ref[...] = acc.reshape(o_ref.shape).astype(o_ref.dtype)

    if nk == 1:
        finish(part)
    else:
        @pl.when(k == 0)
        def _():
            acc_ref[...] = part

        @pl.when(k > 0)
        def _():
            acc_ref[...] += part

        @pl.when(k == nk - 1)
        def _():
            finish(acc_ref[...])


def _weight_spec(b, tk, tn, layer, transposed=False):
    if transposed:
        return pl.BlockSpec((None, tn, tk), lambda i, j, k: (layer, j, k))
    if b.ndim == 2:
        return pl.BlockSpec((tk, tn), lambda i, j, k: (k, j))
    return pl.BlockSpec((None, tk, tn), lambda i, j, k: (layer, k, j))


def matmul(a, b, *, precise, out_dtype=F32, tm=1024, tn=1024, tk=2048, bias=None, silu_a=False,
           out3d_width=None, resid=None, rows_per_gate=None, layer=0, n_cols=None,
           b_transposed=False, name="mm"):
    M, K = a.shape
    N = n_cols if n_cols is not None else (b.shape[-2] if b_transposed else b.shape[-1])
    tm, tn, tk = _pick_tile(M, tm), _pick_tile(N, tn), _pick_tile(K, tk)
    if rows_per_gate is not None:
        tm = _pick_tile(rows_per_gate, tm)
    nk = K // tk
    in_specs = [pl.BlockSpec((tm, tk), lambda i, j, k: (i, k)),
                _weight_spec(b, tk, tn, layer, b_transposed)]
    args = [a, b]
    if bias is not None:
        in_specs.append(pl.BlockSpec((1, tn), lambda i, j, k: (0, j)))
        args.append(bias.reshape(1, N))
    if resid is not None:
        x, g = resid
        in_specs.append(pl.BlockSpec((tm, tn), lambda i, j, k: (i, j)))
        args.append(x)
        if rows_per_gate is None:
            in_specs.append(pl.BlockSpec((tm, tn), lambda i, j, k: (i, j)))
            args.append(g)
        else:
            assert rows_per_gate % tm == 0
            rpt = rows_per_gate // tm
            in_specs.append(pl.BlockSpec((1, 1, tn), lambda i, j, k: (i // rpt, 0, j)))
            args.append(g.reshape(g.shape[0], 1, N))
    if out3d_width is None:
        out_shape = jax.ShapeDtypeStruct((M, N), out_dtype)
        out_spec = pl.BlockSpec((tm, tn), lambda i, j, k: (i, j))
    else:
        assert tn == out3d_width
        out_shape = jax.ShapeDtypeStruct((N // tn, M, tn), out_dtype)
        out_spec = pl.BlockSpec((1, tm, tn), lambda i, j, k: (j, i, 0))
    kern = functools.partial(_mm_kernel, nk=nk, precise=precise, has_bias=bias is not None,
                             silu_a=silu_a, resid=resid is not None, b_transposed=b_transposed)
    return pl.pallas_call(
        kern, grid=(M // tm, N // tn, nk), in_specs=in_specs, out_specs=out_spec,
        out_shape=out_shape, scratch_shapes=[pltpu.VMEM((tm, tn), F32)],
        compiler_params=_cp("parallel", "parallel", "arbitrary"), name=name)(*args)


def _swiglu_kernel(a_ref, wg_ref, wu_ref, o_ref, accg_ref, accu_ref, *, nk, precise):
    k = pl.program_id(2)
    a = a_ref[...]
    pg = _dot(a, wg_ref[...], precise)
    pu = _dot(a, wu_ref[...], precise)

    def finish(g, u):
        o_ref[...] = (g * _sigmoid(g) * u).astype(o_ref.dtype)

    if nk == 1:
        finish(pg, pu)
    else:
        @pl.when(k == 0)
        def _():
            accg_ref[...] = pg
            accu_ref[...] = pu

        @pl.when(k > 0)
        def _():
            accg_ref[...] += pg
            accu_ref[...] += pu

        @pl.when(k == nk - 1)
        def _():
            finish(accg_ref[...], accu_ref[...])


def swiglu_up(a, wg, wu, *, precise, out_dtype, tm=1024, tn=512, tk=2048, layer=0, name="swiglu_up"):
    M, K = a.shape
    N = wg.shape[-1]
    tm, tn, tk = _pick_tile(M, tm), _pick_tile(N, tn), _pick_tile(K, tk)
    nk = K // tk
    kern = functools.partial(_swiglu_kernel, nk=nk, precise=precise)
    return pl.pallas_call(
        kern, grid=(M // tm, N // tn, nk),
        in_specs=[pl.BlockSpec((tm, tk), lambda i, j, k: (i, k)),
                  _weight_spec(wg, tk, tn, layer), _weight_spec(wu, tk, tn, layer)],
        out_specs=pl.BlockSpec((tm, tn), lambda i, j, k: (i, j)),
        out_shape=jax.ShapeDtypeStruct((M, N), out_dtype),
        scratch_shapes=[pltpu.VMEM((tm, tn), F32), pltpu.VMEM((tm, tn), F32)],
        compiler_params=_cp("parallel", "parallel", "arbitrary"), name=name)(a, wg, wu)


def _norm_kernel(*refs, modulated, router, n_experts):
    x_ref, g_ref = refs[0], refs[1]
    i = 2
    if modulated:
        sc_ref, sh_ref = refs[i], refs[i + 1]
        i += 2
    if router:
        wr_ref, br_ref = refs[i], refs[i + 1]
        i += 2
    o_ref = refs[i]
    x = x_ref[...]
    d = x.shape[-1]
    y = x * lax.rsqrt(jnp.mean(x * x, axis=-1, keepdims=True) + EPS) * g_ref[...]
    if modulated:
        sc = sc_ref[...].reshape(-1, d)
        sh = sh_ref[...].reshape(-1, d)
        y = y * (1.0 + sc) + sh
    o_ref[...] = y.astype(o_ref.dtype)
    if router:
        idx_ref, wt_ref = refs[i + 1], refs[i + 2]
        logits = jnp.dot(y, wr_ref[...], precision=HI, preferred_element_type=F32) + br_ref[...]
        lane = lax.broadcasted_iota(jnp.int32, logits.shape, 1)
        neg = jnp.float32(-jnp.inf)
        logits = jnp.where(lane < n_experts, logits, neg)
        m1 = jnp.max(logits, axis=-1, keepdims=True)
        i1 = jnp.min(jnp.where(logits == m1, lane, LANE), axis=-1, keepdims=True)
        rest = jnp.where(lane == i1, neg, logits)
        m2 = jnp.max(rest, axis=-1, keepdims=True)
        i2 = jnp.min(jnp.where(rest == m2, lane, LANE), axis=-1, keepdims=True)
        e2 = jnp.exp(m2 - m1)
        w1 = 1.0 / (1.0 + e2)
        w2 = e2 / (1.0 + e2)
        idx_ref[...] = jnp.where(lane == 0, i1, jnp.where(lane == 1, i2, 0))
        wt_ref[...] = jnp.where(lane == 0, w1, jnp.where(lane == 1, w2, 0.0))


def norm_mod(x, g, sc=None, sh=None, *, out_dtype, rows_per_mod=None, router=None, tm=512,
             name="norm"):
    M, D = x.shape
    tm = _pick_tile(M if rows_per_mod is None else rows_per_mod, tm)
    modulated = sc is not None
    in_specs = [pl.BlockSpec((tm, D), lambda i: (i, 0)), pl.BlockSpec((1, D), lambda i: (0, 0))]
    args = [x, g.reshape(1, D)]
    if modulated:
        if rows_per_mod is None:
            spec = pl.BlockSpec((tm, D), lambda i: (i, 0))
            in_specs += [spec, spec]
            args += [sc, sh]
        else:
            assert rows_per_mod % tm == 0
            rpt = rows_per_mod // tm
            spec = pl.BlockSpec((1, 1, D), lambda i: (i // rpt, 0, 0))
            in_specs += [spec, spec]
            args += [sc.reshape(-1, 1, D), sh.reshape(-1, 1, D)]
    out_shape = [jax.ShapeDtypeStruct((M, D), out_dtype)]
    out_specs = [pl.BlockSpec((tm, D), lambda i: (i, 0))]
    n_experts = 0
    if router is not None:
        w_r, b_r = router
        n_experts = w_r.shape[1]
        w_pad = jnp.zeros((D, LANE), F32).at[:, :n_experts].set(w_r)
        b_pad = jnp.zeros((1, LANE), F32).at[0, :n_experts].set(b_r)
        in_specs += [pl.BlockSpec((D, LANE), lambda i: (0, 0)), pl.BlockSpec((1, LANE), lambda i: (0, 0))]
        args += [w_pad, b_pad]
        out_shape += [jax.ShapeDtypeStruct((M, LANE), jnp.int32), jax.ShapeDtypeStruct((M, LANE), F32)]
        out_specs += [pl.BlockSpec((tm, LANE), lambda i: (i, 0))] * 2
    kern = functools.partial(_norm_kernel, modulated=modulated, router=router is not None,
                             n_experts=n_experts)
    res = pl.pallas_call(kern, grid=(M // tm,), in_specs=in_specs, out_specs=out_specs,
                         out_shape=out_shape, compiler_params=_cp("parallel"), name=name)(*args)
    return res if router is not None else res[0]


def _two_head_q(q):
    lane = lax.broadcasted_iota(jnp.int32, q.shape, 1)
    q0 = jnp.where(lane < 64, q, 0.0).astype(BF16)
    q1 = jnp.where(lane >= 64, q, 0.0).astype(BF16)
    return jnp.concatenate([q0, q1], axis=0)


def _sb_prompt_kernel(q_ref, k_ref, v_ref, g_ref, o_ref, acc_ref, carry_ref, *, t):
    qi = pl.program_id(2)
    qq = _two_head_q(q_ref[0] * (1.0 / math.sqrt(64.0)))
    r = lax.broadcasted_iota(jnp.int32, (t, t), 0)
    c = lax.broadcasted_iota(jnp.int32, (t, t), 1)
    upper = jnp.where(r > c, 1.0, 0.0).astype(BF16)
    upper2 = jnp.concatenate([upper, upper], axis=0)
    strict = jnp.concatenate([c < r, c < r], axis=0)

    def step(j, diagonal):
        start = pl.multiple_of(j * t, t)
        k = k_ref[0, pl.ds(start, t), :].astype(BF16)
        v = v_ref[0, pl.ds(start, t), :].astype(BF16)
        z = lax.dot_general(qq, k, NT_DIMS, preferred_element_type=F32)
        lk = -_softplus(z)
        if diagonal:
            lk = jnp.where(strict, lk, 0.0)
        hi = lk.astype(BF16)
        lo = (lk - hi.astype(F32)).astype(BF16)
        la = jnp.dot(jnp.concatenate([hi, lo], axis=1), upper2, preferred_element_type=F32)
        if diagonal:
            w = jnp.where(strict, jnp.exp(z + lk + la), 0.0)
            carry_ref[...] = jnp.broadcast_to(la[:, 0:1] + lk[:, 0:1], carry_ref.shape)
            acc_ref[...] = jnp.dot(w.astype(BF16), v, preferred_element_type=F32)
        else:
            la = la + carry_ref[:, 0:1]
            w = jnp.exp(z + lk + la)
            carry_ref[...] = jnp.broadcast_to(la[:, 0:1] + lk[:, 0:1], carry_ref.shape)
            acc_ref[...] += jnp.dot(w.astype(BF16), v, preferred_element_type=F32)

    step(qi, True)

    def body(jj, _):
        step(qi - 1 - jj, False)
        return 0

    lax.fori_loop(0, qi, body, 0)
    acc = acc_ref[...]
    lane = lax.broadcasted_iota(jnp.int32, (t, LANE), 1)
    o = jnp.where(lane < 64, acc[:t], acc[t:])
    ms = jnp.dot(o * o, _group_mean_matrix(LANE, 64), precision=HI, preferred_element_type=F32)
    o_ref[...] = (o * lax.rsqrt(ms + EPS) * g_ref[...]).astype(o_ref.dtype)


def sb_attention_prompt(z3, gain, B, T, *, t=256):
    nq = T // t
    n_pairs = z3.shape[2] // LANE
    kern = functools.partial(_sb_prompt_kernel, t=t)
    return pl.pallas_call(
        kern, grid=(B, n_pairs, nq),
        in_specs=[pl.BlockSpec((1, t, LANE), lambda b, h, i: (0, b * nq + i, h)),
                  pl.BlockSpec((1, T, LANE), lambda b, h, i: (1, b, h)),
                  pl.BlockSpec((1, T, LANE), lambda b, h, i: (2, b, h)),
                  pl.BlockSpec((1, LANE), lambda b, h, i: (0, h))],
        out_specs=pl.BlockSpec((t, LANE), lambda b, h, i: (b * nq + i, h)),
        out_shape=jax.ShapeDtypeStruct((B * T, z3.shape[2]), BF16),
        scratch_shapes=[pltpu.VMEM((2 * t, LANE), F32), pltpu.VMEM((2 * t, LANE), F32)],
        compiler_params=_cp("parallel", "parallel", "arbitrary"), name="sb_prompt")(
            z3, z3, z3, gain.reshape(1, -1))


def _diff_lambda(lv_ref):
    lv = lv_ref[...]
    s1 = jnp.sum(lv[0:1] * lv[1:2], axis=-1, keepdims=True)
    s2 = jnp.sum(lv[2:3] * lv[3:4], axis=-1, keepdims=True)
    return jnp.exp(s1) - jnp.exp(s2)


def _diff_prompt_kernel(q_ref, k_ref, v_ref, g_ref, lv_ref, o_ref, acc_ref, m_ref, l_ref, *, t,
                        lam_init):
    h = pl.program_id(1)
    qi = pl.program_id(2)
    qq = _two_head_q(q_ref[0] * (1.0 / math.sqrt(64.0)))
    r = lax.broadcasted_iota(jnp.int32, (2 * t, t), 0)
    r = jnp.where(r >= t, r - t, r)
    c = lax.broadcasted_iota(jnp.int32, (2 * t, t), 1)
    rel = (r - c).astype(F32)
    slope = jnp.exp(jnp.full((1, 1), -2.0 * math.log(2.0), F32) * (h + 1).astype(F32))
    bias0 = slope * rel
    neg = jnp.float32(-jnp.inf)

    def step(j, diagonal):
        start = pl.multiple_of(j * t, t)
        k = k_ref[0, pl.ds(start, t), :].astype(BF16)
        v = v_ref[0, pl.ds(start, t), :].astype(BF16)
        s = lax.dot_general(qq, k, NT_DIMS, preferred_element_type=F32)
        s = s - (bias0 + slope * ((qi - j) * t).astype(F32))
        if diagonal:
            s = jnp.where(c <= r, s, neg)
            m_new = jnp.max(s, axis=-1, keepdims=True)
            p = jnp.exp(s - m_new)
            l_ref[...] = jnp.broadcast_to(jnp.sum(p, axis=-1, keepdims=True), l_ref.shape)
            acc_ref[...] = jnp.dot(p.astype(BF16), v, preferred_element_type=F32)
        else:
            m_old = m_ref[:, 0:1]
            m_new = jnp.maximum(m_old, jnp.max(s, axis=-1, keepdims=True))
            alpha = jnp.exp(m_old - m_new)
            p = jnp.exp(s - m_new)
            l_new = alpha * l_ref[:, 0:1] + jnp.sum(p, axis=-1, keepdims=True)
            l_ref[...] = jnp.broadcast_to(l_new, l_ref.shape)
            acc_ref[...] = alpha * acc_ref[...] + jnp.dot(p.astype(BF16), v, preferred_element_type=F32)
        m_ref[...] = jnp.broadcast_to(m_new, m_ref.shape)

    step(qi, True)

    def body(jj, _):
        step(qi - 1 - jj, False)
        return 0

    lax.fori_loop(0, qi, body, 0)
    lam = _diff_lambda(lv_ref) + lam_init
    on = acc_ref[...] / l_ref[...]
    o = on[:t] - lam * on[t:]
    ms = jnp.mean(o * o, axis=-1, keepdims=True)
    o_ref[...] = (o * lax.rsqrt(ms + EPS) * g_ref[...] * (1.0 - lam_init)).astype(o_ref.dtype)


def diff_attention_prompt(z3, gain, lam_vec, lam_init, B, T, *, t=256):
    nq = T // t
    n_heads = z3.shape[2] // LANE
    kern = functools.partial(_diff_prompt_kernel, t=t, lam_init=lam_init)
    return pl.pallas_call(
        kern, grid=(B, n_heads, nq),
        in_specs=[pl.BlockSpec((1, t, LANE), lambda b, h, i: (3, b * nq + i, h)),
                  pl.BlockSpec((1, T, LANE), lambda b, h, i: (4, b, h)),
                  pl.BlockSpec((1, T, LANE), lambda b, h, i: (5, b, h)),
                  pl.BlockSpec((1, LANE), lambda b, h, i: (0, h)),
                  pl.BlockSpec(lam_vec.shape, lambda b, h, i: (0, 0))],
        out_specs=pl.BlockSpec((t, LANE), lambda b, h, i: (b * nq + i, h)),
        out_shape=jax.ShapeDtypeStruct((B * T, z3.shape[2]), BF16),
        scratch_shapes=[pltpu.VMEM((2 * t, LANE), F32)] * 3,
        compiler_params=_cp("parallel", "parallel", "arbitrary"), name="diff_prompt")(
            z3, z3, z3, gain.reshape(1, -1), lam_vec)


def _gelu_tanh(x):
    return 0.5 * x * (1.0 + jnp.tanh(math.sqrt(2.0 / math.pi) * (x + 0.044715 * (x * x * x))))


def _lru_gates(xc, wr_ref, br_ref, wi_ref, bi_ref, lam_ref):
    r = _sigmoid(jnp.dot(xc, wr_ref[...], precision=HI, preferred_element_type=F32) + br_ref[...])
    i = _sigmoid(jnp.dot(xc, wi_ref[...], precision=HI, preferred_element_type=F32) + bi_ref[...])
    log_a = -LRU_C * r * _softplus(-lam_ref[...])
    a = jnp.exp(log_a)
    u = jnp.sqrt(1.0 - jnp.exp(2.0 * log_a)) * (i * xc)
    return a, u


def _lru_finish(h, gate, g_ref, width):
    y = h * _gelu_tanh(gate)
    ms = jnp.dot(y * y, _group_mean_matrix(width, 64), precision=HI, preferred_element_type=F32)
    return y * lax.rsqrt(ms + EPS) * g_ref[...]


def _lru_prompt_kernel(x_ref, gate_ref, cw_ref, cb_ref, wr_ref, br_ref, wi_ref, bi_ref, lam_ref,
                       g_ref, y_ref, conv_ref, h_ref, xs_ref, hprev_ref, *, tc, width):
    ti = pl.program_id(1)

    @pl.when(ti == 0)
    def _():
        xs_ref[0:8, :] = jnp.zeros((8, width), F32)
        hprev_ref[...] = jnp.zeros_like(hprev_ref)

    @pl.when(ti > 0)
    def _():
        xs_ref[0:8, :] = xs_ref[tc:tc + 8, :]

    x = x_ref[0]
    xs_ref[8:8 + tc, :] = x
    xc = cb_ref[...] + x * cw_ref[3:4, :]
    for j in range(3):
        xc = xc + xs_ref[5 + j:5 + j + tc, :] * cw_ref[j:j + 1, :]
    a, u = _lru_gates(xc, wr_ref, br_ref, wi_ref, bi_ref, lam_ref)
    row = lax.broadcasted_iota(jnp.int32, (tc, width), 0)
    d = 1
    while d < tc:
        valid = row >= d
        a_s = pltpu.roll(a, d, 0)
        u_s = pltpu.roll(u, d, 0)
        u = jnp.where(valid, a * u_s + u, u)
        a = jnp.where(valid, a * a_s, a)
        d *= 2
    h = u + a * hprev_ref[0:1, :]
    hprev_ref[...] = jnp.broadcast_to(h[tc - 1:tc, :], hprev_ref.shape)
    y_ref[...] = _lru_finish(h, gate_ref[0], g_ref, width).astype(y_ref.dtype)

    @pl.when(ti == pl.num_programs(1) - 1)
    def _():
        conv_ref[0] = xs_ref[tc:tc + 8, :]
        h_ref[0] = h[tc - 8:tc, :]


def _block_diag(w):
    n, k, _ = w.shape
    eye = jnp.eye(n, dtype=w.dtype)
    return jnp.einsum('nkj,nm->nkmj', w, eye).reshape(n * k, n * k)


def lru_prompt(z3, p, l, B, T, *, tc=256):
    width = z3.shape[2]
    nt = T // tc
    row = lambda a: a.reshape(1, width)
    const = lambda shape: pl.BlockSpec(shape, lambda b, i: (0,) * len(shape))
    kern = functools.partial(_lru_prompt_kernel, tc=tc, width=width)
    y, conv, h = pl.pallas_call(
        kern, grid=(B, nt),
        in_specs=[pl.BlockSpec((1, tc, width), lambda b, i: (6, b * nt + i, 0)),
                  pl.BlockSpec((1, tc, width), lambda b, i: (7, b * nt + i, 0)),
                  const((4, width)), const((1, width)), const((width, width)), const((1, width)),
                  const((width, width)), const((1, width)), const((1, width)), const((1, width))],
        out_specs=[pl.BlockSpec((tc, width), lambda b, i: (b * nt + i, 0)),
                   pl.BlockSpec((1, 8, width), lambda b, i: (b, 0, 0)),
                   pl.BlockSpec((1, 8, width), lambda b, i: (b, 0, 0))],
        out_shape=[jax.ShapeDtypeStruct((B * T, width), BF16),
                   jax.ShapeDtypeStruct((B, 8, width), F32),
                   jax.ShapeDtypeStruct((B, 8, width), F32)],
        scratch_shapes=[pltpu.VMEM((tc + 8, width), F32), pltpu.VMEM((8, width), F32)],
        compiler_params=_cp("parallel", "arbitrary"), name="lru_prompt")(
            z3, z3, p['lru_conv_w'][l], row(p['lru_conv_b'][l]), _block_diag(p['lru_w_r'][l]),
            row(p['lru_b_r'][l]), _block_diag(p['lru_w_i'][l]), row(p['lru_b_i'][l]),
            row(p['lru_lam'][l]), row(p['g_lru_out'][l]))
    return y, conv[:, 5:8, :], h[:, 7, :]


def _lru_step_kernel(x_ref, gate_ref, conv_ref, h0_ref, cw_ref, cb_ref, wr_ref, br_ref, wi_ref,
                     bi_ref, lam_ref, g_ref, y_ref, h_ref, *, width):
    xc = cb_ref[...] + x_ref[...] * cw_ref[3:4, :]
    for j in range(3):
        xc = xc + conv_ref[j] * cw_ref[j:j + 1, :]
    a, u = _lru_gates(xc, wr_ref, br_ref, wi_ref, bi_ref, lam_ref)
    h = a * h0_ref[...] + u
    h_ref[...] = h
    y_ref[...] = _lru_finish(h, gate_ref[...], g_ref, width)


def lru_step(x, gate, conv0, h0, p, l):
    G, width = x.shape
    row = lambda a: a.reshape(1, width)
    kern = functools.partial(_lru_step_kernel, width=width)
    y, h = pl.pallas_call(
        kern, out_shape=[jax.ShapeDtypeStruct((G, width), F32)] * 2, name="lru_step")(
            x, gate, conv0.swapaxes(0, 1), h0, p['lru_conv_w'][l], row(p['lru_conv_b'][l]),
            _block_diag(p['lru_w_r'][l]), row(p['lru_b_r'][l]), _block_diag(p['lru_w_i'][l]),
            row(p['lru_b_i'][l]), row(p['lru_lam'][l]), row(p['g_lru_out'][l]))
    conv_new = jnp.concatenate([conv0[:, 1:], x[:, None, :]], axis=1)
    return y, conv_new, h


def _log_sigmoid(x):
    return -_softplus(-x)


def _mlstm_prompt_kernel(q_ref, k_ref, v_ref, og_ref, gc_ref, gr_ref, g_ref, y_ref, c_out, n_out,
                         m_out, c_ref, n_ref, m_ref, *, L, nc, d):
    c_ref[...] = jnp.zeros_like(c_ref)
    n_ref[...] = jnp.zeros_like(n_ref)
    m_ref[...] = jnp.zeros_like(m_ref)
    r = lax.broadcasted_iota(jnp.int32, (L, L), 0)
    cc = lax.broadcasted_iota(jnp.int32, (L, L), 1)
    causal = cc <= r
    neg = jnp.float32(-jnp.inf)

    def chunk(ci, _):
        start = pl.multiple_of(ci * L, L)
        q = q_ref[0, pl.ds(start, L), :]
        ks = k_ref[0, pl.ds(start, L), :] * (1.0 / math.sqrt(d))
        v = v_ref[0, pl.ds(start, L), :]
        gcol = gc_ref[0, 0, pl.ds(start, L), :]
        grow = gr_ref[0, 0, :, pl.ds(start, L)]
        i_col, i_row = gcol[:, 0:1], grow[0:1, :]
        lf_col, lf_row = _log_sigmoid(gcol[:, 1:2]), _log_sigmoid(grow[1:2, :])
        b_col = jnp.sum(jnp.where(causal, lf_row, 0.0), axis=-1, keepdims=True)
        b_row = jnp.sum(jnp.where(r <= cc, lf_col, 0.0), axis=0, keepdims=True)
        m_prev = m_ref[0:1, 0:1]
        dmat = jnp.where(causal, b_col - b_row + i_row, neg)
        m_t = jnp.maximum(b_col + m_prev, jnp.max(dmat, axis=-1, keepdims=True))
        w = jnp.exp(dmat - m_t)
        inter = jnp.exp(b_col + m_prev - m_t)
        s = lax.dot_general(q, ks, NT_DIMS, precision=HI, preferred_element_type=F32)
        sw = w * s
        c = c_ref[...]
        n = n_ref[0:1, :]
        num = (jnp.dot(sw, v, precision=HI, preferred_element_type=F32)
               + inter * lax.dot_general(q, c, NT_DIMS, precision=HI, preferred_element_type=F32))
        den = jnp.sum(sw, axis=-1, keepdims=True) + inter * jnp.sum(q * n, axis=-1, keepdims=True)
        h = num / jnp.maximum(jnp.abs(den), jnp.exp(-m_t))
        m_new = m_t[L - 1:L, :]
        decay = inter[L - 1:L, :]
        w_last = jnp.exp(b_col[L - 1:L, :] - b_col + i_col - m_new)
        vw = v * w_last
        c_ref[...] = decay * c + lax.dot_general(vw, ks, (((0,), (0,)), ((), ())), precision=HI,
                                                 preferred_element_type=F32)
        n_ref[...] = jnp.broadcast_to(decay * n + jnp.sum(ks * w_last, axis=0, keepdims=True),
                                      n_ref.shape)
        m_ref[...] = jnp.broadcast_to(m_new, m_ref.shape)
        y = h * _sigmoid(og_ref[0, pl.ds(start, L), :])
        ms = jnp.mean(y * y, axis=-1, keepdims=True)
        y_ref[pl.ds(start, L), :] = (y * lax.rsqrt(ms + EPS) * g_ref[...]).astype(y_ref.dtype)
        return 0

    lax.fori_loop(0, nc, chunk, 0)
    c_out[0, 0] = c_ref[...]
    n_out[0, 0] = n_ref[...]
    m_out[0, 0] = m_ref[...]


def mlstm_prompt(z3, gates, gain, B, T, *, L=128):
    H = z3.shape[2] // LANE
    d = LANE
    g4 = gates.reshape(B, T, 2, H)
    gcol = g4.transpose(0, 3, 1, 2)
    grow = g4.transpose(0, 3, 2, 1)
    kern = functools.partial(_mlstm_prompt_kernel, L=L, nc=T // L, d=d)
    col = lambda grp: pl.BlockSpec((1, T, LANE), lambda b, h: (grp, b, h))
    y, c, n, m = pl.pallas_call(
        kern, grid=(B, H),
        in_specs=[col(8), col(9), col(10), col(11),
                  pl.BlockSpec((1, 1, T, 2), lambda b, h: (b, h, 0, 0)),
                  pl.BlockSpec((1, 1, 2, T), lambda b, h: (b, h, 0, 0)),
                  pl.BlockSpec((1, LANE), lambda b, h: (0, h))],
        out_specs=[pl.BlockSpec((T, LANE), lambda b, h: (b, h)),
                   pl.BlockSpec((1, 1, d, d), lambda b, h: (b, h, 0, 0)),
                   pl.BlockSpec((1, 1, 8, d), lambda b, h: (b, h, 0, 0)),
                   pl.BlockSpec((1, 1, 8, LANE), lambda b, h: (b, h, 0, 0))],
        out_shape=[jax.ShapeDtypeStruct((B * T, H * d), BF16),
                   jax.ShapeDtypeStruct((B, H, d, d), F32),
                   jax.ShapeDtypeStruct((B, H, 8, d), F32),
                   jax.ShapeDtypeStruct((B, H, 8, LANE), F32)],
        scratch_shapes=[pltpu.VMEM((d, d), F32), pltpu.VMEM((8, d), F32), pltpu.VMEM((8, LANE), F32)],
        compiler_params=_cp("parallel", "parallel"), name="mlstm_prompt")(
            z3, z3, z3, z3, gcol, grow, gain.reshape(1, -1))
    return y, c, n[:, :, 0, :], m[:, :, 0, 0]


def _mlstm_step_kernel(q_ref, k_ref, v_ref, og_ref, i_ref, f_ref, c_ref, n_ref, m_ref, g_ref,
                       y_ref, c_out, n_out, m_out, *, d):
    q, v = q_ref[0], v_ref[0]
    k = k_ref[0] * (1.0 / math.sqrt(d))
    i_pre, lf = i_ref[0], _log_sigmoid(f_ref[0])
    c, n, m = c_ref[0], n_ref[0], m_ref[0]
    m_t = jnp.maximum(lf + m, i_pre)
    w = jnp.exp(i_pre - m_t)
    inter = jnp.exp(lf + m - m_t)
    sw = w * jnp.sum(q * k, axis=-1, keepdims=True)
    eye = (lax.broadcasted_iota(jnp.int32, (1, d, d), 1) == lax.broadcasted_iota(jnp.int32, (1, d, d), 2))
    cq_col = jnp.sum(c * q, axis=-1, keepdims=True)
    cq = jnp.sum(jnp.where(eye, cq_col, 0.0), axis=1, keepdims=True)
    num = sw * v + inter * cq
    den = sw + inter * jnp.sum(n * q, axis=-1, keepdims=True)
    h = num / jnp.maximum(jnp.abs(den), jnp.exp(-m_t))
    v_col = jnp.sum(jnp.where(eye, v, 0.0), axis=-1, keepdims=True)
    c_out[0] = inter * c + (w * v_col) * k
    n_out[0] = inter * n + w * k
    m_out[0] = m_t
    y = h * _sigmoid(og_ref[0])
    ms = jnp.mean(y * y, axis=-1, keepdims=True)
    y_ref[0] = y * lax.rsqrt(ms + EPS) * g_ref[...]


def mlstm_step(q, k, v, og, i_pre, f_pre, c0, n0, m0, gain):
    G, H, d = n0.shape
    vec = lambda a: a.reshape(G, H, 1, d)
    sca = lambda a: a.reshape(G, H, 1, 1)
    vspec = pl.BlockSpec((1, H, 1, d), lambda b: (b, 0, 0, 0))
    sspec = pl.BlockSpec((1, H, 1, 1), lambda b: (b, 0, 0, 0))
    cspec = pl.BlockSpec((1, H, d, d), lambda b: (b, 0, 0, 0))
    kern = functools.partial(_mlstm_step_kernel, d=d)
    y, c, n, m = pl.pallas_call(
        kern, grid=(G,),
        in_specs=[vspec, vspec, vspec, vspec, sspec, sspec, cspec, vspec, sspec,
                  pl.BlockSpec((H, 1, d), lambda b: (0, 0, 0))],
        out_specs=[vspec, cspec, vspec, sspec],
        out_shape=[jax.ShapeDtypeStruct((G, H, 1, d), F32), jax.ShapeDtypeStruct((G, H, d, d), F32),
                   jax.ShapeDtypeStruct((G, H, 1, d), F32), jax.ShapeDtypeStruct((G, H, 1, 1), F32)],
        compiler_params=_cp("parallel"), name="mlstm_step")(
            vec(q), vec(k), vec(v), vec(og), sca(i_pre), sca(f_pre), c0, vec(n0), sca(m0),
            gain.reshape(H, 1, d))
    return y.reshape(G, H * d), c, n.reshape(G, H, d), m.reshape(G, H)


def _page_specs(P, n_pages, block):
    zeros = (0,) * (len(block) - 1)

    def spec(p):
        return pl.BlockSpec(block, lambda b, s, rows: (rows[b * n_pages + s * P + p],) + zeros)
    return [spec(p) for p in range(P)]


def _decode_scores_kernel(rows_ref, q_ref, *rest, P, page):
    pages, o_ref = rest[:P], rest[P]
    n_grp, d = q_ref.shape[1], q_ref.shape[2]
    q = jnp.broadcast_to(q_ref[0] * (1.0 / math.sqrt(d)), (n_grp, d, page))
    for p in range(P):
        o_ref[0, :, :, p * page:(p + 1) * page] = jnp.sum(pages[p][0] * q, axis=1, keepdims=True)


def decode_scores(q, pool_t, rows, n_pages, *, P=8):
    G = q.shape[0]
    _, n_grp, d, page = pool_t.shape
    P = _pick_tile(n_pages, P)
    kern = functools.partial(_decode_scores_kernel, P=P, page=page)
    grid_spec = pltpu.PrefetchScalarGridSpec(
        num_scalar_prefetch=1, grid=(G, n_pages // P),
        in_specs=[pl.BlockSpec((1, n_grp, d, 1), lambda b, s, rows: (b, 0, 0, 0))]
        + _page_specs(P, n_pages, (1, n_grp, d, page)),
        out_specs=pl.BlockSpec((1, n_grp, 1, P * page), lambda b, s, rows: (b, 0, 0, s)))
    return pl.pallas_call(
        kern, grid_spec=grid_spec,
        out_shape=jax.ShapeDtypeStruct((G, n_grp, 1, n_pages * page), F32),
        compiler_params=_cp("parallel", "arbitrary"), name="decode_scores")(
            rows, q.reshape(G, n_grp, d, 1), *([pool_t] * P))


def _sb_weights_kernel(z_ref, w_ref, *, n_tiles):
    bb = z_ref.shape[0]
    r = lax.broadcasted_iota(jnp.int32, (LANE, LANE), 0)
    c = lax.broadcasted_iota(jnp.int32, (LANE, LANE), 1)
    upper = jnp.where(r > c, 1.0, 0.0).astype(F32)
    carry = jnp.zeros((bb * 8, 1), F32)
    for t in reversed(range(n_tiles)):
        z = z_ref[:, :, t * LANE:(t + 1) * LANE].reshape(bb * 8, LANE)
        lk = -_softplus(z)
        la = jnp.dot(lk, upper, precision=HI, preferred_element_type=F32) + carry
        w_ref[:, :, t * LANE:(t + 1) * LANE] = jnp.exp(z + lk + la).reshape(bb, 8, LANE)
        carry = carry + jnp.sum(lk, axis=-1, keepdims=True)


def sb_weights(z, *, bb=8):
    G, _, Tp = z.shape
    bb = _pick_tile(G, bb)
    kern = functools.partial(_sb_weights_kernel, n_tiles=Tp // LANE)
    spec = pl.BlockSpec((bb, 8, Tp), lambda i: (i, 0, 0))
    return pl.pallas_call(kern, grid=(G // bb,), in_specs=[spec], out_specs=spec,
                          out_shape=jax.ShapeDtypeStruct(z.shape, F32),
                          compiler_params=_cp("parallel"), name="sb_weights")(z)


def _diff_weights_kernel(z_ref, q_ref, k_ref, v_ref, lv_ref, w_ref, new_ref, *, lam_init):
    bb, _, Tp = z_ref.shape
    width = q_ref.shape[-1]
    lam = (_diff_lambda(lv_ref) + lam_init).reshape(1, 1, 1)
    prod = q_ref[...] * k_ref[...] * (1.0 / math.sqrt(64.0))
    grp = lax.broadcasted_iota(jnp.int32, (1, 8, width), 2) // 64
    r8 = lax.broadcasted_iota(jnp.int32, (1, 8, width), 1)
    mine = grp == (r8 % 4) * 2 + r8 // 4
    s_new = jnp.sum(jnp.where(mine, prod, 0.0), axis=-1, keepdims=True)
    head = lax.broadcasted_iota(jnp.int32, (1, 8, 1), 1) % 4
    slope = jnp.exp((-2.0 * math.log(2.0)) * (head + 1).astype(F32))
    kpos = lax.broadcasted_iota(jnp.int32, (1, 1, Tp), 2)
    s = z_ref[...] - slope * (Tp - kpos).astype(F32)
    m = jnp.maximum(jnp.max(s, axis=-1, keepdims=True), s_new)
    p = jnp.exp(s - m)
    p_new = jnp.exp(s_new - m)
    den = jnp.sum(p, axis=-1, keepdims=True) + p_new
    p = p / den
    p_new = p_new / den
    w = p[:, 0:4, :] - lam * p[:, 4:8, :]
    w8 = jnp.concatenate([w, jnp.zeros_like(w)], axis=1).reshape(bb * 8, Tp)
    n_heads = 4
    spread = (lax.broadcasted_iota(jnp.int32, (LANE, n_heads * LANE), 1) // n_heads
              == lax.broadcasted_iota(jnp.int32, (LANE, n_heads * LANE), 0)).astype(F32)
    keep = (lax.broadcasted_iota(jnp.int32, (bb * 8, n_heads * LANE), 1) % n_heads
            == lax.broadcasted_iota(jnp.int32, (bb * 8, n_heads * LANE), 0) % 8)
    for t in range(Tp // LANE):
        wide = jnp.dot(w8[:, t * LANE:(t + 1) * LANE], spread, precision=HI,
                       preferred_element_type=F32)
        w_ref[:, :, t * n_heads * LANE:(t + 1) * n_heads * LANE] = jnp.where(keep, wide, 0.0).reshape(
            bb, 8, n_heads * LANE)
    w_new = p_new[:, 0:4, :] - lam * p_new[:, 4:8, :]
    new_ref[...] = w_new * v_ref[...]


def diff_weights(z, q, k_new, v_new, lam_vec, lam_init, *, bb=8):
    G, _, Tp = z.shape
    width = q.shape[-1]
    bb = _pick_tile(G, bb)
    kern = functools.partial(_diff_weights_kernel, lam_init=lam_init)
    zspec = pl.BlockSpec((bb, 8, Tp), lambda i: (i, 0, 0))
    qspec = pl.BlockSpec((bb, 1, width), lambda i: (i, 0, 0))
    vspec = pl.BlockSpec((bb, 4, LANE), lambda i: (i, 0, 0))
    r3 = lambda a: a.reshape(G, 1, width)
    return pl.pallas_call(
        kern, grid=(G // bb,),
        in_specs=[zspec, qspec, qspec, vspec, pl.BlockSpec(lam_vec.shape, lambda i: (0, 0))],
        out_specs=[pl.BlockSpec((bb, 8, 4 * Tp), lambda i: (i, 0, 0)), vspec],
        out_shape=[jax.ShapeDtypeStruct((G, 8, 4 * Tp), F32), jax.ShapeDtypeStruct((G, 4, LANE), F32)],
        compiler_params=_cp("parallel"), name="diff_weights")(
            z, r3(q), r3(k_new), v_new.reshape(G, 4, LANE), lam_vec)


def _decode_pv_t_kernel(rows_ref, w_ref, *rest, P, page):
    pages, o_ref, acc_ref = rest[:P], rest[P], rest[P + 1]
    s = pl.program_id(1)

    @pl.when(s == 0)
    def _():
        acc_ref[...] = jnp.zeros_like(acc_ref)

    acc = acc_ref[...]
    for p in range(P):
        acc = acc + pages[p][0] * w_ref[0, :, :, p * page:(p + 1) * page]
    acc_ref[...] = acc

    @pl.when(s == pl.num_programs(1) - 1)
    def _():
        d = acc.shape[1]
        col = jnp.sum(acc, axis=-1, keepdims=True)
        eye = (lax.broadcasted_iota(jnp.int32, (1, d, d), 1)
               == lax.broadcasted_iota(jnp.int32, (1, d, d), 2))
        o_ref[0] = jnp.sum(jnp.where(eye, col, 0.0), axis=1, keepdims=True)


def decode_pv_t(w, pool_t, rows, n_pages, *, P=8):
    G = w.shape[0]
    _, n_grp, d, page = pool_t.shape
    P = _pick_tile(n_pages, P)
    kern = functools.partial(_decode_pv_t_kernel, P=P, page=page)
    grid_spec = pltpu.PrefetchScalarGridSpec(
        num_scalar_prefetch=1, grid=(G, n_pages // P),
        in_specs=[pl.BlockSpec((1, n_grp, 1, P * page), lambda b, s, rows: (b, 0, 0, s))]
        + _page_specs(P, n_pages, (1, n_grp, d, page)),
        out_specs=pl.BlockSpec((1, n_grp, 1, d), lambda b, s, rows: (b, 0, 0, 0)),
        scratch_shapes=[pltpu.VMEM((n_grp, d, page), F32)])
    out = pl.pallas_call(
        kern, grid_spec=grid_spec, out_shape=jax.ShapeDtypeStruct((G, n_grp, 1, d), F32),
        compiler_params=_cp("parallel", "arbitrary"), name="decode_pv_t")(rows, w, *([pool_t] * P))
    return out.reshape(G, n_grp * d)


def _decode_pv_rows_kernel(rows_ref, w_ref, init_ref, *rest, P, n_rows):
    pages, o_ref, acc_ref = rest[:P], rest[P], rest[P + 1]
    s = pl.program_id(1)

    @pl.when(s == 0)
    def _():
        acc_ref[...] = jnp.zeros_like(acc_ref)

    acc = acc_ref[...]
    for p in range(P):
        acc = acc + jnp.dot(w_ref[0, :, p * n_rows:(p + 1) * n_rows], pages[p][0], precision=HI,
                            preferred_element_type=F32)
    acc_ref[...] = acc

    @pl.when(s == pl.num_programs(1) - 1)
    def _():
        o_ref[0] = acc[0:4] + init_ref[0]


def decode_pv_rows(w, pool, rows, init, n_pages, *, P=8):
    G = w.shape[0]
    _, n_rows, width = pool.shape
    P = _pick_tile(n_pages, P)
    kern = functools.partial(_decode_pv_rows_kernel, P=P, n_rows=n_rows)
    grid_spec = pltpu.PrefetchScalarGridSpec(
        num_scalar_prefetch=1, grid=(G, n_pages // P),
        in_specs=[pl.BlockSpec((1, 8, P * n_rows), lambda b, s, rows: (b, 0, s)),
                  pl.BlockSpec((1, 4, width), lambda b, s, rows: (b, 0, 0))]
        + _page_specs(P, n_pages, (1, n_rows, width)),
        out_specs=pl.BlockSpec((1, 4, width), lambda b, s, rows: (b, 0, 0)),
        scratch_shapes=[pltpu.VMEM((8, width), F32)])
    out = pl.pallas_call(
        kern, grid_spec=grid_spec, out_shape=jax.ShapeDtypeStruct((G, 4, width), F32),
        compiler_params=_cp("parallel", "arbitrary"), name="decode_pv_rows")(
            rows, w, init, *([pool] * P))
    return out.reshape(G, 4 * width)


def _group_norm_kernel(x_ref, g_ref, o_ref, *, group, post_scale):
    x = x_ref[...]
    ms = jnp.dot(x * x, _group_mean_matrix(x.shape[-1], group), precision=HI,
                 preferred_element_type=F32)
    o_ref[...] = x * lax.rsqrt(ms + EPS) * g_ref[...] * post_scale


def _moe_up_kernel(te_ref, act_ref, a_ref, wg_ref, wu_ref, o_ref):
    i = pl.program_id(1)

    @pl.when(act_ref[i] > 0)
    def _():
        a = a_ref[...]
        g = jnp.dot(a, wg_ref[...], preferred_element_type=F32)
        u = jnp.dot(a, wu_ref[...], preferred_element_type=F32)
        o_ref[...] = (g * _sigmoid(g) * u).astype(o_ref.dtype)

    @pl.when(act_ref[i] == 0)
    def _():
        o_ref[...] = jnp.zeros_like(o_ref)


def _moe_down_kernel(te_ref, act_ref, a_ref, wd_ref, o_ref):
    i = pl.program_id(1)

    @pl.when(act_ref[i] > 0)
    def _():
        o_ref[...] = jnp.dot(a_ref[...], wd_ref[...], preferred_element_type=F32)

    @pl.when(act_ref[i] == 0)
    def _():
        o_ref[...] = jnp.zeros_like(o_ref)


def moe_grouped(xg, tile_expert, tile_active, wg, wu, wd, *, tg, tn_up=1408, tn_down=1024):
    n_rows, D = xg.shape
    F = wg.shape[-1]
    n_tiles = n_rows // tg
    tn_up, tn_down = _pick_tile(F, tn_up), _pick_tile(D, tn_down)
    up_spec = pltpu.PrefetchScalarGridSpec(
        num_scalar_prefetch=2, grid=(F // tn_up, n_tiles),
        in_specs=[pl.BlockSpec((tg, D), lambda j, i, te, ac: (i, 0)),
                  pl.BlockSpec((None, D, tn_up), lambda j, i, te, ac: (te[i], 0, j)),
                  pl.BlockSpec((None, D, tn_up), lambda j, i, te, ac: (te[i], 0, j))],
        out_specs=pl.BlockSpec((tg, tn_up), lambda j, i, te, ac: (i, j)))
    act = pl.pallas_call(_moe_up_kernel, grid_spec=up_spec,
                         out_shape=jax.ShapeDtypeStruct((n_rows, F), BF16),
                         compiler_params=_cp("parallel", "arbitrary"), name="moe_up")(
                             tile_expert, tile_active, xg, wg, wu)
    down_spec = pltpu.PrefetchScalarGridSpec(
        num_scalar_prefetch=2, grid=(D // tn_down, n_tiles),
        in_specs=[pl.BlockSpec((tg, F), lambda j, i, te, ac: (i, 0)),
                  pl.BlockSpec((None, F, tn_down), lambda j, i, te, ac: (te[i], 0, j))],
        out_specs=pl.BlockSpec((tg, tn_down), lambda j, i, te, ac: (i, j)))
    return pl.pallas_call(_moe_down_kernel, grid_spec=down_spec,
                          out_shape=jax.ShapeDtypeStruct((n_rows, D), F32),
                          compiler_params=_cp("parallel", "arbitrary"), name="moe_down")(
                              tile_expert, tile_active, act, wd)


def _moe_dispatch(idx, n_experts, tg):
    M = idx.shape[0]
    flat_e = idx.reshape(-1)
    onehot = (flat_e[:, None] == jnp.arange(n_experts, dtype=jnp.int32)[None, :]).astype(jnp.int32)
    counts = jnp.sum(onehot, axis=0)
    rank = jnp.sum((jnp.cumsum(onehot, axis=0) - onehot) * onehot, axis=1)
    padded = (counts + tg - 1) // tg * tg
    ends = jnp.cumsum(padded)
    pos = (ends - padded)[flat_e] + rank
    n_rows = 2 * M + n_experts * tg
    row_token = jnp.zeros((n_rows,), jnp.int32).at[pos].set(jnp.arange(2 * M, dtype=jnp.int32) // 2)
    tile_start = jnp.arange(n_rows // tg, dtype=jnp.int32) * tg
    tile_expert = jnp.minimum(jnp.searchsorted(ends, tile_start, side='right'), n_experts - 1)
    tile_active = (tile_start < ends[-1]).astype(jnp.int32)
    return pos.reshape(M, 2), row_token, tile_expert.astype(jnp.int32), tile_active


def _moe_combine_kernel(x_ref, g_ref, y1_ref, y2_ref, w_ref, o_ref):
    w = w_ref[...]
    f = w[:, 0:1] * y1_ref[...] + w[:, 1:2] * y2_ref[...]
    o_ref[...] = x_ref[...] + g_ref[0] * f


def moe_combine(x, g, y1, y2, wts, rows_per_gate, *, tm=512):
    M, D = x.shape
    tm = _pick_tile(rows_per_gate, tm)
    rpt = rows_per_gate // tm
    row = pl.BlockSpec((tm, D), lambda i: (i, 0))
    return pl.pallas_call(
        _moe_combine_kernel, grid=(M // tm,),
        in_specs=[row, pl.BlockSpec((1, 1, D), lambda i: (i // rpt, 0, 0)), row, row,
                  pl.BlockSpec((tm, LANE), lambda i: (i, 0))],
        out_specs=row, out_shape=jax.ShapeDtypeStruct((M, D), F32),
        compiler_params=_cp("parallel"), name="moe_combine")(x, g.reshape(-1, 1, D), y1, y2, wts)


def _moe_dense_up_kernel(a_ref, wg_ref, wu_ref, o_ref, accg_ref, accu_ref, *, nk):
    k = pl.program_id(2)
    a = a_ref[...]
    pg = jnp.dot(a, wg_ref[...], precision=HI, preferred_element_type=F32)
    pu = jnp.dot(a, wu_ref[...], precision=HI, preferred_element_type=F32)

    @pl.when(k == 0)
    def _():
        accg_ref[...] = pg
        accu_ref[...] = pu

    @pl.when(k > 0)
    def _():
        accg_ref[...] += pg
        accu_ref[...] += pu

    @pl.when(k == nk - 1)
    def _():
        g = accg_ref[...]
        o_ref[...] = g * _sigmoid(g) * accu_ref[...]


def _moe_dense_down_kernel(a_ref, wd_ref, cw_ref, x_ref, g_ref, o_ref, acc_ref):
    e, k = pl.program_id(0), pl.program_id(1)
    part = cw_ref[...] * jnp.dot(a_ref[...], wd_ref[...], precision=HI, preferred_element_type=F32)

    @pl.when((e == 0) & (k == 0))
    def _():
        acc_ref[...] = part

    @pl.when((e > 0) | (k > 0))
    def _():
        acc_ref[...] += part

    @pl.when((e == pl.num_programs(0) - 1) & (k == pl.num_programs(1) - 1))
    def _():
        o_ref[...] = x_ref[...] + g_ref[...] * acc_ref[...]


def moe_dense(h, x, gate, combine, wg, wu, wd, *, tn=1408, tk=1024, tkd=1408):
    M, D = h.shape
    E, _, F = wg.shape
    tn, tk, tkd = _pick_tile(F, tn), _pick_tile(D, tk), _pick_tile(F, tkd)
    nk = D // tk
    wspec = pl.BlockSpec((None, tk, tn), lambda e, j, k: (e, k, j))
    act = pl.pallas_call(
        functools.partial(_moe_dense_up_kernel, nk=nk), grid=(E, F // tn, nk),
        in_specs=[pl.BlockSpec((M, tk), lambda e, j, k: (0, k)), wspec, wspec],
        out_specs=pl.BlockSpec((None, M, tn), lambda e, j, k: (e, 0, j)),
        out_shape=jax.ShapeDtypeStruct((E, M, F), F32),
        scratch_shapes=[pltpu.VMEM((M, tn), F32), pltpu.VMEM((M, tn), F32)],
        compiler_params=_cp("parallel", "parallel", "arbitrary"), name="moe_dense_up")(h, wg, wu)
    full = pl.BlockSpec((M, D), lambda e, k: (0, 0))
    return pl.pallas_call(
        _moe_dense_down_kernel, grid=(E, F // tkd),
        in_specs=[pl.BlockSpec((None, M, tkd), lambda e, k: (e, 0, k)),
                  pl.BlockSpec((None, tkd, D), lambda e, k: (e, k, 0)),
                  pl.BlockSpec((None, M, 1), lambda e, k: (e, 0, 0)), full, full],
        out_specs=full, out_shape=jax.ShapeDtypeStruct((M, D), F32),
        scratch_shapes=[pltpu.VMEM((M, D), F32)],
        compiler_params=_cp("arbitrary", "arbitrary"), name="moe_dense_down")(
            act, wd, combine, x, gate)


def group_norm(x, gain, *, group, post_scale=1.0):
    kern = functools.partial(_group_norm_kernel, group=group, post_scale=post_scale)
    return pl.pallas_call(kern, out_shape=jax.ShapeDtypeStruct(x.shape, F32), name="group_norm")(
        x, gain.reshape(1, -1))


W_GROUP = 512
N_MAIN = 12 * W_GROUP
MOE_TILE = 256


def _lam_init(l):
    return 0.8 - 0.6 * math.exp(-0.3 * l)


def _gate_weight(w_in_l):
    wg = w_in_l[:, N_MAIN:]
    return jnp.zeros((wg.shape[0], LANE), F32).at[:, :wg.shape[1]].set(wg)


def _prompt_trunk(x3, mods, p, wb):
    B, T, D = x3.shape
    M = B * T
    x = x3.reshape(M, D)
    news = []
    for l in range(len(mods)):
        sh1, sc1, g1, sh2, sc2, g2 = mods[l]
        h = norm_mod(x, p['g_norm1'][l], sc1, sh1, out_dtype=BF16, rows_per_mod=T, name="norm1_p")
        z3 = matmul(h, wb['w_in'], precise=False, layer=l, n_cols=N_MAIN, out3d_width=W_GROUP,
                    tm=1024, tn=W_GROUP, name="w_in_p")
        gates = matmul(h, _gate_weight(p['w_in'][l]).astype(BF16), precise=False, name="gates_p")
        gates = gates[:, :8] + jnp.concatenate([p['ml_b_i'][l], p['ml_b_f'][l]])[None, :]
        y_sb = sb_attention_prompt(z3, p['g_sb_out'][l], B, T)
        y_df = diff_attention_prompt(z3, p['g_diff_out'][l], p['diff_lam'][l], _lam_init(l), B, T)
        y_lru, conv_new, h_new = lru_prompt(z3, p, l, B, T)
        y_ml, c_new, n_new, m_new = mlstm_prompt(z3, gates, p['g_ml_out'][l], B, T)
        ycat = jnp.concatenate([y_sb, y_df, y_lru, y_ml], axis=-1)
        x = matmul(ycat, wb['w_out'], precise=False, layer=l, resid=(x, g1), rows_per_gate=T,
                   name="w_out_p")
        j = l // 2
        if l % 2 == 0:
            h2 = norm_mod(x, p['g_norm2'][l], sc2, sh2, out_dtype=BF16, rows_per_mod=T, name="norm2_p")
            act = swiglu_up(h2, wb['ffn_w_gate'], wb['ffn_w_up'], precise=False, out_dtype=BF16,
                            layer=j, name="ffn_up_p")
            x = matmul(act, wb['ffn_w_down'], precise=False, layer=j, tk=1408, resid=(x, g2),
                       rows_per_gate=T, name="ffn_down_p")
        else:
            h2, idx, wts = norm_mod(x, p['g_norm2'][l], sc2, sh2, out_dtype=BF16, rows_per_mod=T,
                                    router=(p['moe_w_router'][j], p['moe_b_router'][j]),
                                    name="norm2_router_p")
            n_experts = p['moe_w_router'].shape[-1]
            pos, row_token, tile_expert, tile_active = _moe_dispatch(idx[:, :2], n_experts, MOE_TILE)
            xg = jnp.take(h2, row_token, axis=0)
            yg = moe_grouped(xg, tile_expert, tile_active, wb['moe_w_gate'][j], wb['moe_w_up'][j],
                             wb['moe_w_down'][j], tg=MOE_TILE)
            x = moe_combine(x, g2, jnp.take(yg, pos[:, 0], axis=0), jnp.take(yg, pos[:, 1], axis=0),
                            wts, T)
        news.append((z3[1].reshape(B, T, 8, 64), z3[2].reshape(B, T, 8, 64),
                     z3[4].reshape(B, T, 4, 2, 64), z3[5].reshape(B, T, 4, 128),
                     conv_new, h_new, c_new, n_new, m_new))
    y = norm_mod(x, p['g_final'], out_dtype=F32, name="final_norm_p").reshape(B, T, D)
    return y, tuple(jnp.stack([nw[i] for nw in news], axis=0) for i in range(9))


def _sample_trunk(x3, mods, p, caches, states, page_table):
    G, _, D = x3.shape
    x = x3.reshape(G, D)
    sb_k_c, sb_v_c, df_k_c, df_v_c = caches
    n_layers, n_phys, page = sb_k_c.shape[:3]
    n_pool = n_layers * n_phys
    pool_sb_k = sb_k_c.transpose(0, 1, 3, 4, 2).reshape(n_pool, 8, 64, page)
    pool_sb_v = sb_v_c.transpose(0, 1, 3, 4, 2).reshape(n_pool, 8, 64, page)
    pool_df_k = df_k_c.transpose(0, 1, 3, 4, 5, 2).reshape(n_pool, 8, 64, page)
    pool_df_v = df_v_c.reshape(n_pool, page * 4, LANE)
    n_pages = page_table.shape[1]
    conv_all, h_all, c_all, n_all, m_all = states
    news = []
    for l in range(len(mods)):
        sh1, sc1, g1, sh2, sc2, g2 = mods[l]
        rows = (page_table + l * n_phys).reshape(-1).astype(jnp.int32)
        h = norm_mod(x, p['g_norm1'][l], sc1, sh1, out_dtype=F32, name="norm1_s")
        z = matmul(h, p['w_in'].swapaxes(1, 2), precise=True, layer=l, n_cols=N_MAIN, tn=W_GROUP,
                   b_transposed=True, name="w_in_s")
        gates = matmul(h, _gate_weight(p['w_in'][l]), precise=True, name="gates_s")
        zs = [z[:, i * W_GROUP:(i + 1) * W_GROUP] for i in range(12)]
        sb_q, sb_k, sb_v, df_q, df_k, df_v, lru_x, lru_g, ml_q, ml_k, ml_v, ml_o = zs
        ml_i = gates[:, 0:4] + p['ml_b_i'][l][None, :]
        ml_f = gates[:, 4:8] + p['ml_b_f'][l][None, :]
        Tp = n_pages * page
        z_sb = decode_scores(sb_q, pool_sb_k, rows, n_pages).reshape(G, 8, Tp)
        w_sb = sb_weights(z_sb).reshape(G, 8, 1, Tp)
        y_sb = decode_pv_t(w_sb, pool_sb_v, rows, n_pages)
        y_sb = group_norm(y_sb, p['g_sb_out'][l], group=64)
        z_df = decode_scores(df_q, pool_df_k, rows, n_pages)
        z_df = z_df.reshape(G, 4, 2, Tp).swapaxes(1, 2).reshape(G, 8, Tp)
        w_df, new_df = diff_weights(z_df, df_q, df_k, df_v, p['diff_lam'][l], _lam_init(l))
        y_df = decode_pv_rows(w_df, pool_df_v, rows, new_df, n_pages)
        y_df = group_norm(y_df, p['g_diff_out'][l], group=128, post_scale=1.0 - _lam_init(l))
        y_lru, conv_new, h_new = lru_step(lru_x, lru_g, conv_all[l], h_all[l], p, l)
        y_ml, c_new, n_new, m_new = mlstm_step(ml_q, ml_k, ml_v, ml_o, ml_i, ml_f, c_all[l],
                                               n_all[l], m_all[l], p['g_ml_out'][l])
        ycat = jnp.concatenate([y_sb, y_df, y_lru, y_ml], axis=-1)
        x = matmul(ycat, p['w_out'], precise=True, layer=l, resid=(x, g1), name="w_out_s")
        j = l // 2
        if l % 2 == 0:
            h2 = norm_mod(x, p['g_norm2'][l], sc2, sh2, out_dtype=F32, name="norm2_s")
            act = swiglu_up(h2, p['ffn_w_gate'], p['ffn_w_up'], precise=True, out_dtype=F32,
                            layer=j, tk=1024, name="ffn_up_s")
            x = matmul(act, p['ffn_w_down'], precise=True, layer=j, tk=1408, resid=(x, g2),
                       name="ffn_down_s")
        else:
            h2, idx, wts = norm_mod(x, p['g_norm2'][l], sc2, sh2, out_dtype=F32,
                                    router=(p['moe_w_router'][j], p['moe_b_router'][j]),
                                    name="norm2_router_s")
            n_experts = p['moe_w_router'].shape[-1]
            e_ids = jnp.arange(n_experts, dtype=jnp.int32)[:, None]
            combine = (jnp.where(idx[None, :, 0] == e_ids, wts[None, :, 0], 0.0)
                       + jnp.where(idx[None, :, 1] == e_ids, wts[None, :, 1], 0.0))[..., None]
            x = moe_dense(h2, x, g2, combine, p['moe_w_gate'][j], p['moe_w_up'][j], p['moe_w_down'][j])
        news.append((sb_k.reshape(G, 1, 8, 64), sb_v.reshape(G, 1, 8, 64),
                     df_k.reshape(G, 1, 4, 2, 64), df_v.reshape(G, 1, 4, 128),
                     conv_new, h_new, c_new, n_new, m_new))
    y = norm_mod(x, p['g_final'], out_dtype=F32, name="final_norm_s").reshape(G, 1, D)
    return y, tuple(jnp.stack([nw[i] for nw in news], axis=0) for i in range(9))


def kernel(x_prompt, x_sample, cache_sb_k, cache_sb_v, cache_diff_k, cache_diff_v,
           state_lru_conv, state_lru_h, state_mlstm_c, state_mlstm_n, state_mlstm_m,
           page_table, c_prompt, c_sample, w_ada, b_ada, g_norm1, g_norm2, w_in, w_out,
           g_sb_out, diff_lam, g_diff_out, lru_conv_w, lru_conv_b, lru_w_r, lru_b_r,
           lru_w_i, lru_b_i, lru_lam, g_lru_out, ml_b_i, ml_b_f, g_ml_out,
           ffn_w_gate, ffn_w_up, ffn_w_down, moe_w_router, moe_b_router,
           moe_w_gate, moe_w_up, moe_w_down, g_final):
    p = dict(w_ada=w_ada, b_ada=b_ada, g_norm1=g_norm1, g_norm2=g_norm2, w_in=w_in, w_out=w_out,
             g_sb_out=g_sb_out, diff_lam=diff_lam, g_diff_out=g_diff_out, lru_conv_w=lru_conv_w,
             lru_conv_b=lru_conv_b, lru_w_r=lru_w_r, lru_b_r=lru_b_r, lru_w_i=lru_w_i,
             lru_b_i=lru_b_i, lru_lam=lru_lam, g_lru_out=g_lru_out, ml_b_i=ml_b_i, ml_b_f=ml_b_f,
             g_ml_out=g_ml_out, ffn_w_gate=ffn_w_gate, ffn_w_up=ffn_w_up, ffn_w_down=ffn_w_down,
             moe_w_router=moe_w_router, moe_b_router=moe_b_router, moe_w_gate=moe_w_gate,
             moe_w_up=moe_w_up, moe_w_down=moe_w_down, g_final=g_final)
    depth, D = g_norm1.shape
    Bp, Gs = x_prompt.shape[0], x_sample.shape[0]
    n_c = Bp + Gs
    c_all = jnp.zeros(((n_c + 7) // 8 * 8, D), F32).at[:Bp].set(c_prompt).at[Bp:n_c].set(c_sample)
    mods_p, mods_s = [], []
    for l in range(depth):
        m = matmul(c_all, w_ada, precise=True, layer=l, bias=b_ada[l], silu_a=True, name="ada")
        mods_p.append([m[:Bp, i * D:(i + 1) * D] for i in range(6)])
        mods_s.append([m[Bp:n_c, i * D:(i + 1) * D] for i in range(6)])
    wb = {k: p[k].astype(BF16) for k in ('w_in', 'w_out', 'ffn_w_gate', 'ffn_w_up', 'ffn_w_down',
                                         'moe_w_gate', 'moe_w_up', 'moe_w_down')}
    y_p, new_p = _prompt_trunk(x_prompt, mods_p, p, wb)
    y_s, new_s = _sample_trunk(x_sample, mods_s, p,
                               (cache_sb_k, cache_sb_v, cache_diff_k, cache_diff_v),
                               (state_lru_conv, state_lru_h, state_mlstm_c, state_mlstm_n,
                                state_mlstm_m), page_table)
    return (y_p, y_s) + new_p + new_s
```

```python
import functools
import math

import jax
import jax.numpy as jnp
from jax import lax
from jax.experimental import pallas as pl
from jax.experimental.pallas import tpu as pltpu

F32 = jnp.float32
BF16 = jnp.bfloat16
HI = lax.Precision.HIGHEST
EPS = 1e-6
LRU_C = 8.0
LANE = 128
VMEM_LIMIT = 56 * 1024 * 1024

NT_DIMS = (((1,), (1,)), ((), ()))


def _cp(*sem):
    return pltpu.CompilerParams(dimension_semantics=sem, vmem_limit_bytes=VMEM_LIMIT)


def _split(x):
    hi = x.astype(BF16)
    return hi, (x - hi.astype(F32)).astype(BF16)


def _dot3(a, b, dims=(((1,), (0,)), ((), ()))):
    m = a.shape[0]
    a_hi = a.astype(BF16)
    a_parts = jnp.concatenate([a, a - a_hi.astype(F32)], axis=0).astype(BF16)
    b_hi, b_lo = _split(b)
    r = lax.dot_general(a_parts, b_hi, dims, preferred_element_type=F32)
    return r[:m] + r[m:] + lax.dot_general(a_hi, b_lo, dims, preferred_element_type=F32)


def _dot(a, b, precise):
    if precise:
        return _dot3(a, b)
    return jnp.dot(a.astype(BF16), b.astype(BF16), preferred_element_type=F32)


def _softplus(x):
    return jnp.maximum(x, 0.0) + jnp.log1p(jnp.exp(-jnp.abs(x)))


def _sigmoid(x):
    return 1.0 / (1.0 + jnp.exp(-x))


def _group_mean_matrix(width, group):
    r = lax.broadcasted_iota(jnp.int32, (width, width), 0) // group
    c = lax.broadcasted_iota(jnp.int32, (width, width), 1) // group
    return jnp.where(r == c, 1.0 / group, 0.0).astype(F32)


def _pick_tile(n, pref):
    t = min(n, pref)
    while n % t:
        t //= 2
    return t


def _mm_kernel(*refs, nk, precise, has_bias, silu_a, resid, b_transposed):
    a_ref, b_ref = refs[0], refs[1]
    i = 2
    bias_ref = x_ref = g_ref = None
    if has_bias:
        bias_ref = refs[i]
        i += 1
    if resid:
        x_ref, g_ref = refs[i], refs[i + 1]
        i += 2
    o_ref, acc_ref = refs[i], refs[i + 1]
    k = pl.program_id(2)
    a = a_ref[...]
    if silu_a:
        a = a * _sigmoid(a)
    if b_transposed:
        assert precise
        part = _dot3(a, b_ref[...], NT_DIMS)
    else:
        part = _dot(a, b_ref[...], precise)

    def finish(acc):
        if has_bias:
            acc = acc + bias_ref[...]
        if resid:
            g = g_ref[...].reshape(-1, acc.shape[-1])
            acc = x_ref[...] + g * acc
        o_ref[...] = acc.reshape(o_ref.shape).astype(o_ref.dtype)

    if nk == 1:
        finish(part)
    else:
        @pl.when(k == 0)
        def _():
            acc_ref[...] = part

        @pl.when(k > 0)
        def _():
            acc_ref[...] += part

        @pl.when(k == nk - 1)
        def _():
            finish(acc_ref[...])


def _weight_spec(b, tk, tn, layer, transposed=False):
    if transposed:
        return pl.BlockSpec((None, tn, tk), lambda i, j, k: (layer, j, k))
    if b.ndim == 2:
        return pl.BlockSpec((tk, tn), lambda i, j, k: (k, j))
    return pl.BlockSpec((None, tk, tn), lambda i, j, k: (layer, k, j))


def matmul(a, b, *, precise, out_dtype=F32, tm=1024, tn=1024, tk=2048, bias=None, silu_a=False,
           out3d_width=None, resid=None, rows_per_gate=None, layer=0, n_cols=None,
           b_transposed=False, name="mm"):
    M, K = a.shape
    N = n_cols if n_cols is not None else (b.shape[-2] if b_transposed else b.shape[-1])
    tm, tn, tk = _pick_tile(M, tm), _pick_tile(N, tn), _pick_tile(K, tk)
    if rows_per_gate is not None:
        tm = _pick_tile(rows_per_gate, tm)
    nk = K // tk
    in_specs = [pl.BlockSpec((tm, tk), lambda i, j, k: (i, k)),
                _weight_spec(b, tk, tn, layer, b_transposed)]
    args = [a, b]
    if bias is not None:
        in_specs.append(pl.BlockSpec((1, tn), lambda i, j, k: (0, j)))
        args.append(bias.reshape(1, N))
    if resid is not None:
        x, g = resid
        in_specs.append(pl.BlockSpec((tm, tn), lambda i, j, k: (i, j)))
        args.append(x)
        if rows_per_gate is None:
            in_specs.append(pl.BlockSpec((tm, tn), lambda i, j, k: (i, j)))
            args.append(g)
        else:
            assert rows_per_gate % tm == 0
            rpt = rows_per_gate // tm
            in_specs.append(pl.BlockSpec((1, 1, tn), lambda i, j, k: (i // rpt, 0, j)))
            args.append(g.reshape(g.shape[0], 1, N))
    if out3d_width is None:
        out_shape = jax.ShapeDtypeStruct((M, N), out_dtype)
        out_spec = pl.BlockSpec((tm, tn), lambda i, j, k: (i, j))
    else:
        assert tn == out3d_width
        out_shape = jax.ShapeDtypeStruct((N // tn, M, tn), out_dtype)
        out_spec = pl.BlockSpec((1, tm, tn), lambda i, j, k: (j, i, 0))
    kern = functools.partial(_mm_kernel, nk=nk, precise=precise, has_bias=bias is not None,
                             silu_a=silu_a, resid=resid is not None, b_transposed=b_transposed)
    return pl.pallas_call(
        kern, grid=(M // tm, N // tn, nk), in_specs=in_specs, out_specs=out_spec,
        out_shape=out_shape, scratch_shapes=[pltpu.VMEM((tm, tn), F32)],
        compiler_params=_cp("parallel", "parallel", "arbitrary"), name=name)(*args)


def _swiglu_kernel(a_ref, wg_ref, wu_ref, o_ref, accg_ref, accu_ref, *, nk, precise):
    k = pl.program_id(2)
    a = a_ref[...]
    pg = _dot(a, wg_ref[...], precise)
    pu = _dot(a, wu_ref[...], precise)

    def finish(g, u):
        o_ref[...] = (g * _sigmoid(g) * u).astype(o_ref.dtype)

    if nk == 1:
        finish(pg, pu)
    else:
        @pl.when(k == 0)
        def _():
            accg_ref[...] = pg
            accu_ref[...] = pu

        @pl.when(k > 0)
        def _():
            accg_ref[...] += pg
            accu_ref[...] += pu

        @pl.when(k == nk - 1)
        def _():
            finish(accg_ref[...], accu_ref[...])


def swiglu_up(a, wg, wu, *, precise, out_dtype, tm=1024, tn=512, tk=2048, layer=0, name="swiglu_up"):
    M, K = a.shape
    N = wg.shape[-1]
    tm, tn, tk = _pick_tile(M, tm), _pick_tile(N, tn), _pick_tile(K, tk)
    nk = K // tk
    kern = functools.partial(_swiglu_kernel, nk=nk, precise=precise)
    return pl.pallas_call(
        kern, grid=(M // tm, N // tn, nk),
        in_specs=[pl.BlockSpec((tm, tk), lambda i, j, k: (i, k)),
                  _weight_spec(wg, tk, tn, layer), _weight_spec(wu, tk, tn, layer)],
        out_specs=pl.BlockSpec((tm, tn), lambda i, j, k: (i, j)),
        out_shape=jax.ShapeDtypeStruct((M, N), out_dtype),
        scratch_shapes=[pltpu.VMEM((tm, tn), F32), pltpu.VMEM((tm, tn), F32)],
        compiler_params=_cp("parallel", "parallel", "arbitrary"), name=name)(a, wg, wu)


def _norm_kernel(*refs, modulated, router, n_experts):
    x_ref, g_ref = refs[0], refs[1]
    i = 2
    if modulated:
        sc_ref, sh_ref = refs[i], refs[i + 1]
        i += 2
    if router:
        wr_ref, br_ref = refs[i], refs[i + 1]
        i += 2
    o_ref = refs[i]
    x = x_ref[...]
    d = x.shape[-1]
    y = x * lax.rsqrt(jnp.mean(x * x, axis=-1, keepdims=True) + EPS) * g_ref[...]
    if modulated:
        sc = sc_ref[...].reshape(-1, d)
        sh = sh_ref[...].reshape(-1, d)
        y = y * (1.0 + sc) + sh
    o_ref[...] = y.astype(o_ref.dtype)
    if router:
        idx_ref, wt_ref = refs[i + 1], refs[i + 2]
        logits = jnp.dot(y, wr_ref[...], precision=HI, preferred_element_type=F32) + br_ref[...]
        lane = lax.broadcasted_iota(jnp.int32, logits.shape, 1)
        neg = jnp.float32(-jnp.inf)
        logits = jnp.where(lane < n_experts, logits, neg)
        m1 = jnp.max(logits, axis=-1, keepdims=True)
        i1 = jnp.min(jnp.where(logits == m1, lane, LANE), axis=-1, keepdims=True)
        rest = jnp.where(lane == i1, neg, logits)
        m2 = jnp.max(rest, axis=-1, keepdims=True)
        i2 = jnp.min(jnp.where(rest == m2, lane, LANE), axis=-1, keepdims=True)
        e2 = jnp.exp(m2 - m1)
        w1 = 1.0 / (1.0 + e2)
        w2 = e2 / (1.0 + e2)
        idx_ref[...] = jnp.where(lane == 0, i1, jnp.where(lane == 1, i2, 0))
        wt_ref[...] = jnp.where(lane == 0, w1, jnp.where(lane == 1, w2, 0.0))


def norm_mod(x, g, sc=None, sh=None, *, out_dtype, rows_per_mod=None, router=None, tm=512,
             name="norm"):
    M, D = x.shape
    tm = _pick_tile(M if rows_per_mod is None else rows_per_mod, tm)
    modulated = sc is not None
    in_specs = [pl.BlockSpec((tm, D), lambda i: (i, 0)), pl.BlockSpec((1, D), lambda i: (0, 0))]
    args = [x, g.reshape(1, D)]
    if modulated:
        if rows_per_mod is None:
            spec = pl.BlockSpec((tm, D), lambda i: (i, 0))
            in_specs += [spec, spec]
            args += [sc, sh]
        else:
            assert rows_per_mod % tm == 0
            rpt = rows_per_mod // tm
            spec = pl.BlockSpec((1, 1, D), lambda i: (i // rpt, 0, 0))
            in_specs += [spec, spec]
            args += [sc.reshape(-1, 1, D), sh.reshape(-1, 1, D)]
    out_shape = [jax.ShapeDtypeStruct((M, D), out_dtype)]
    out_specs = [pl.BlockSpec((tm, D), lambda i: (i, 0))]
    n_experts = 0
    if router is not None:
        w_r, b_r = router
        n_experts = w_r.shape[1]
        w_pad = jnp.zeros((D, LANE), F32).at[:, :n_experts].set(w_r)
        b_pad = jnp.zeros((1, LANE), F32).at[0, :n_experts].set(b_r)
        in_specs += [pl.BlockSpec((D, LANE), lambda i: (0, 0)), pl.BlockSpec((1, LANE), lambda i: (0, 0))]
        args += [w_pad, b_pad]
        out_shape += [jax.ShapeDtypeStruct((M, LANE), jnp.int32), jax.ShapeDtypeStruct((M, LANE), F32)]
        out_specs += [pl.BlockSpec((tm, LANE), lambda i: (i, 0))] * 2
    kern = functools.partial(_norm_kernel, modulated=modulated, router=router is not None,
                             n_experts=n_experts)
    res = pl.pallas_call(kern, grid=(M // tm,), in_specs=in_specs, out_specs=out_specs,
                         out_shape=out_shape, compiler_params=_cp("parallel"), name=name)(*args)
    return res if router is not None else res[0]


def _two_head_q(q):
    lane = lax.broadcasted_iota(jnp.int32, q.shape, 1)
    q0 = jnp.where(lane < 64, q, 0.0).astype(BF16)
    q1 = jnp.where(lane >= 64, q, 0.0).astype(BF16)
    return jnp.concatenate([q0, q1], axis=0)


def _sb_prompt_kernel(q_ref, k_ref, v_ref, g_ref, o_ref, acc_ref, carry_ref, *, t, n_pairs):
    qi = pl.program_id(1)
    r = lax.broadcasted_iota(jnp.int32, (t, t), 0)
    c = lax.broadcasted_iota(jnp.int32, (t, t), 1)
    upper = jnp.where(r > c, 1.0, 0.0).astype(BF16)
    upper2 = jnp.concatenate([upper, upper], axis=0)
    strict = jnp.concatenate([c < r, c < r], axis=0)
    n_tiles = t // LANE

    def lanes(x):
        return jnp.concatenate([x] * n_tiles, axis=1) if n_tiles > 1 else x
    qqs = [_two_head_q(q_ref[0, :, hp * LANE:(hp + 1) * LANE] * (1.0 / math.sqrt(64.0)))
           for hp in range(n_pairs)]

    def step(j, diagonal):
        start = pl.multiple_of(j * t, t)
        for hp in range(n_pairs):
            cols = slice(hp * LANE, (hp + 1) * LANE)
            k = k_ref[0, pl.ds(start, t), cols].astype(BF16)
            v = v_ref[0, pl.ds(start, t), cols].astype(BF16)
            z = lax.dot_general(qqs[hp], k, NT_DIMS, preferred_element_type=F32)
            lk = jnp.minimum(-z, 0.0) - jnp.log(1.0 + jnp.exp(-jnp.abs(z)))
            if diagonal:
                lk = jnp.where(strict, lk, 0.0)
            hi = lk.astype(BF16)
            lo = (lk - hi.astype(F32)).astype(BF16)
            la = jnp.dot(jnp.concatenate([hi, lo], axis=1), upper2, preferred_element_type=F32)
            total = jnp.broadcast_to(la[:, 0:1] + lk[:, 0:1], (2 * t, LANE))
            if diagonal:
                w = jnp.where(strict, jnp.exp(z + lk + la), 0.0)
                carry_ref[hp] = total
                acc_ref[hp] = jnp.dot(w.astype(BF16), v, preferred_element_type=F32)
            else:
                carry = carry_ref[hp]
                w = jnp.exp(z + lk + la + lanes(carry))
                carry_ref[hp] = carry + total
                acc_ref[hp] += jnp.dot(w.astype(BF16), v, preferred_element_type=F32)

    step(qi, True)

    def body(jj, _):
        step(qi - 1 - jj, False)
        return 0

    lax.fori_loop(0, qi, body, 0)
    lane = lax.broadcasted_iota(jnp.int32, (t, LANE), 1)
    mean64 = _group_mean_matrix(LANE, 64)
    for hp in range(n_pairs):
        acc = acc_ref[hp]
        o = jnp.where(lane < 64, acc[:t], acc[t:])
        ms = jnp.dot(o * o, mean64, precision=HI, preferred_element_type=F32)
        cols = slice(hp * LANE, (hp + 1) * LANE)
        o_ref[:, cols] = (o * lax.rsqrt(ms + EPS) * g_ref[:, cols]).astype(o_ref.dtype)


def sb_attention_prompt(z3, gain, B, T, *, t=256):
    nq = T // t
    width = z3.shape[2]
    n_pairs = width // LANE
    kern = functools.partial(_sb_prompt_kernel, t=t, n_pairs=n_pairs)
    return pl.pallas_call(
        kern, grid=(B, nq),
        in_specs=[pl.BlockSpec((1, t, width), lambda b, i: (0, b * nq + i, 0)),
                  pl.BlockSpec((1, T, width), lambda b, i: (1, b, 0)),
                  pl.BlockSpec((1, T, width), lambda b, i: (2, b, 0)),
                  pl.BlockSpec((1, width), lambda b, i: (0, 0))],
        out_specs=pl.BlockSpec((t, width), lambda b, i: (b * nq + i, 0)),
        out_shape=jax.ShapeDtypeStruct((B * T, width), BF16),
        scratch_shapes=[pltpu.VMEM((n_pairs, 2 * t, LANE), F32)] * 2,
        compiler_params=_cp("parallel", "arbitrary"), name="sb_prompt")(
            z3, z3, z3, gain.reshape(1, -1))


def _diff_lambda(lv_ref):
    lv = lv_ref[...]
    s1 = jnp.sum(lv[0:1] * lv[1:2], axis=-1, keepdims=True)
    s2 = jnp.sum(lv[2:3] * lv[3:4], axis=-1, keepdims=True)
    return jnp.exp(s1) - jnp.exp(s2)


def _diff_prompt_kernel(q_ref, k_ref, v_ref, g_ref, lv_ref, o_ref, acc_ref, m_ref, *, t,
                        n_heads, lam_init):
    qi = pl.program_id(1)
    assert t <= 256
    r = lax.broadcasted_iota(jnp.int32, (2 * t, t), 0)
    r = jnp.where(r >= t, r - t, r)
    c = lax.broadcasted_iota(jnp.int32, (2 * t, t), 1)
    causal = c <= r
    kc = lax.broadcasted_iota(jnp.int32, (t, LANE), 0)
    kl = lax.broadcasted_iota(jnp.int32, (t, LANE), 1)
    k_pos = jnp.where(kl == 0, kc // 16, jnp.where(kl == 1, kc % 16, 0)).astype(BF16)
    ql = lax.broadcasted_iota(jnp.int32, (2 * t, LANE), 1)
    neg = jnp.float32(-jnp.inf)
    slopes = [2.0 ** (-8.0 * (h + 1) / n_heads) for h in range(n_heads)]
    q_augs = []
    for h in range(n_heads):
        qq = _two_head_q(q_ref[0, :, h * LANE:(h + 1) * LANE] * (1.0 / math.sqrt(64.0)))
        q_pos = jnp.where(ql == 0, 16.0 * slopes[h], jnp.where(ql == 1, slopes[h], 0.0)).astype(BF16)
        q_augs.append(jnp.concatenate([qq, q_pos], axis=1))

    ones = jnp.ones((t, LANE), BF16)
    n_tiles = t // LANE

    def lanes(x):
        return jnp.concatenate([x] * n_tiles, axis=1) if n_tiles > 1 else x

    def row_max(s):
        part = s[:, 0:LANE]
        for i in range(1, n_tiles):
            part = jnp.maximum(part, s[:, i * LANE:(i + 1) * LANE])
        return jnp.broadcast_to(jnp.max(part, axis=-1, keepdims=True), part.shape)

    def step(j, diagonal):
        start = pl.multiple_of(j * t, t)
        for h in range(n_heads):
            cols = slice(h * LANE, (h + 1) * LANE)
            k = jnp.concatenate([k_ref[0, pl.ds(start, t), cols].astype(BF16), k_pos], axis=1)
            v = jnp.concatenate([v_ref[0, pl.ds(start, t), cols].astype(BF16), ones], axis=1)
            s = lax.dot_general(q_augs[h], k, NT_DIMS, preferred_element_type=F32)
            off = slopes[h] * (j * t).astype(F32)
            if diagonal:
                s = jnp.where(causal, s, neg)
                m_loc = row_max(s)
                p = jnp.exp(s - lanes(m_loc))
                m_ref[h] = m_loc + off
                acc_ref[h] = jnp.dot(p.astype(BF16), v, preferred_element_type=F32)
            else:
                m_old = m_ref[h]
                m_new = jnp.maximum(m_old, row_max(s) + off)
                alpha = jnp.exp(m_old - m_new)
                p = jnp.exp(s - lanes(m_new - off))
                m_ref[h] = m_new
                acc_ref[h] = (jnp.concatenate([alpha, alpha], axis=1) * acc_ref[h]
                              + jnp.dot(p.astype(BF16), v, preferred_element_type=F32))

    step(qi, True)

    def body(jj, _):
        step(qi - 1 - jj, False)
        return 0

    lax.fori_loop(0, qi, body, 0)
    lam = _diff_lambda(lv_ref) + lam_init
    for h in range(n_heads):
        cols = slice(h * LANE, (h + 1) * LANE)
        acc = acc_ref[h]
        on = acc[:, :LANE] / acc[:, LANE:]
        o = on[:t] - lam * on[t:]
        ms = jnp.mean(o * o, axis=-1, keepdims=True)
        o_ref[:, cols] = (o * lax.rsqrt(ms + EPS) * g_ref[:, cols] * (1.0 - lam_init)).astype(o_ref.dtype)


def diff_attention_prompt(z3, gain, lam_vec, lam_init, B, T, *, t=256):
    nq = T // t
    width = z3.shape[2]
    n_heads = width // LANE
    kern = functools.partial(_diff_prompt_kernel, t=t, n_heads=n_heads, lam_init=lam_init)
    return pl.pallas_call(
        kern, grid=(B, nq),
        in_specs=[pl.BlockSpec((1, t, width), lambda b, i: (3, b * nq + i, 0)),
                  pl.BlockSpec((1, T, width), lambda b, i: (4, b, 0)),
                  pl.BlockSpec((1, T, width), lambda b, i: (5, b, 0)),
                  pl.BlockSpec((1, width), lambda b, i: (0, 0)),
                  pl.BlockSpec(lam_vec.shape, lambda b, i: (0, 0))],
        out_specs=pl.BlockSpec((t, width), lambda b, i: (b * nq + i, 0)),
        out_shape=jax.ShapeDtypeStruct((B * T, width), BF16),
        scratch_shapes=[pltpu.VMEM((n_heads, 2 * t, 2 * LANE), F32),
                        pltpu.VMEM((n_heads, 2 * t, LANE), F32)],
        compiler_params=_cp("parallel", "arbitrary"), name="diff_prompt")(
            z3, z3, z3, gain.reshape(1, -1), lam_vec)


def _gelu_tanh(x):
    return 0.5 * x * (1.0 + jnp.tanh(math.sqrt(2.0 / math.pi) * (x + 0.044715 * (x * x * x))))


def _lru_gates(xc, wr_ref, br_ref, wi_ref, bi_ref, lam_ref):
    r = _sigmoid(jnp.dot(xc, wr_ref[...], precision=HI, preferred_element_type=F32) + br_ref[...])
    i = _sigmoid(jnp.dot(xc, wi_ref[...], precision=HI, preferred_element_type=F32) + bi_ref[...])
    log_a = -LRU_C * r * _softplus(-lam_ref[...])
    a = jnp.exp(log_a)
    u = jnp.sqrt(1.0 - jnp.exp(2.0 * log_a)) * (i * xc)
    return a, u


def _lru_finish(h, gate, g_ref, width):
    y = h * _gelu_tanh(gate)
    ms = jnp.dot(y * y, _group_mean_matrix(width, 64), precision=HI, preferred_element_type=F32)
    return y * lax.rsqrt(ms + EPS) * g_ref[...]


def _lru_prompt_kernel(x_ref, gate_ref, cw_ref, cb_ref, wr_ref, br_ref, wi_ref, bi_ref, lam_ref,
                       g_ref, y_ref, conv_ref, h_ref, xs_ref, hprev_ref, *, tc, width):
    ti = pl.program_id(1)

    @pl.when(ti == 0)
    def _():
        xs_ref[0:8, :] = jnp.zeros((8, width), F32)
        hprev_ref[...] = jnp.zeros_like(hprev_ref)

    @pl.when(ti > 0)
    def _():
        xs_ref[0:8, :] = xs_ref[tc:tc + 8, :]

    x = x_ref[0]
    xs_ref[8:8 + tc, :] = x
    xc = cb_ref[...] + x * cw_ref[3:4, :]
    for j in range(3):
        xc = xc + xs_ref[5 + j:5 + j + tc, :] * cw_ref[j:j + 1, :]
    a, u = _lru_gates(xc, wr_ref, br_ref, wi_ref, bi_ref, lam_ref)
    row = lax.broadcasted_iota(jnp.int32, (tc, width), 0)
    d = 1
    while d < tc:
        valid = row >= d
        a_s = pltpu.roll(a, d, 0)
        u_s = pltpu.roll(u, d, 0)
        u = jnp.where(valid, a * u_s + u, u)
        a = jnp.where(valid, a * a_s, a)
        d *= 2
    h = u + a * hprev_ref[0:1, :]
    hprev_ref[...] = jnp.broadcast_to(h[tc - 1:tc, :], hprev_ref.shape)
    y_ref[...] = _lru_finish(h, gate_ref[0], g_ref, width).astype(y_ref.dtype)

    @pl.when(ti == pl.num_programs(1) - 1)
    def _():
        conv_ref[0] = xs_ref[tc:tc + 8, :]
        h_ref[0] = h[tc - 8:tc, :]


def _block_diag(w):
    n, k, _ = w.shape
    eye = jnp.eye(n, dtype=w.dtype)
    return jnp.einsum('nkj,nm->nkmj', w, eye).reshape(n * k, n * k)


def lru_prompt(z3, p, l, B, T, *, tc=256):
    width = z3.shape[2]
    nt = T // tc
    row = lambda a: a.reshape(1, width)
    const = lambda shape: pl.BlockSpec(shape, lambda b, i: (0,) * len(shape))
    kern = functools.partial(_lru_prompt_kernel, tc=tc, width=width)
    y, conv, h = pl.pallas_call(
        kern, grid=(B, nt),
        in_specs=[pl.BlockSpec((1, tc, width), lambda b, i: (6, b * nt + i, 0)),
                  pl.BlockSpec((1, tc, width), lambda b, i: (7, b * nt + i, 0)),
                  const((4, width)), const((1, width)), const((width, width)), const((1, width)),
                  const((width, width)), const((1, width)), const((1, width)), const((1, width))],
        out_specs=[pl.BlockSpec((tc, width), lambda b, i: (b * nt + i, 0)),
                   pl.BlockSpec((1, 8, width), lambda b, i: (b, 0, 0)),
                   pl.BlockSpec((1, 8, width), lambda b, i: (b, 0, 0))],
        out_shape=[jax.ShapeDtypeStruct((B * T, width), BF16),
                   jax.ShapeDtypeStruct((B, 8, width), F32),
                   jax.ShapeDtypeStruct((B, 8, width), F32)],
        scratch_shapes=[pltpu.VMEM((tc + 8, width), F32), pltpu.VMEM((8, width), F32)],
        compiler_params=_cp("parallel", "arbitrary"), name="lru_prompt")(
            z3, z3, p['lru_conv_w'][l], row(p['lru_conv_b'][l]), _block_diag(p['lru_w_r'][l]),
            row(p['lru_b_r'][l]), _block_diag(p['lru_w_i'][l]), row(p['lru_b_i'][l]),
            row(p['lru_lam'][l]), row(p['g_lru_out'][l]))
    return y, conv[:, 5:8, :], h[:, 7, :]


def _lru_step_kernel(x_ref, gate_ref, conv_ref, h0_ref, cw_ref, cb_ref, wr_ref, br_ref, wi_ref,
                     bi_ref, lam_ref, g_ref, y_ref, h_ref, *, width):
    xc = cb_ref[...] + x_ref[...] * cw_ref[3:4, :]
    for j in range(3):
        xc = xc + conv_ref[j] * cw_ref[j:j + 1, :]
    a, u = _lru_gates(xc, wr_ref, br_ref, wi_ref, bi_ref, lam_ref)
    h = a * h0_ref[...] + u
    h_ref[...] = h
    y_ref[...] = _lru_finish(h, gate_ref[...], g_ref, width)


def lru_step(x, gate, conv0, h0, p, l):
    G, width = x.shape
    row = lambda a: a.reshape(1, width)
    kern = functools.partial(_lru_step_kernel, width=width)
    y, h = pl.pallas_call(
        kern, out_shape=[jax.ShapeDtypeStruct((G, width), F32)] * 2, name="lru_step")(
            x, gate, conv0.swapaxes(0, 1), h0, p['lru_conv_w'][l], row(p['lru_conv_b'][l]),
            _block_diag(p['lru_w_r'][l]), row(p['lru_b_r'][l]), _block_diag(p['lru_w_i'][l]),
            row(p['lru_b_i'][l]), row(p['lru_lam'][l]), row(p['g_lru_out'][l]))
    conv_new = jnp.concatenate([conv0[:, 1:], x[:, None, :]], axis=1)
    return y, conv_new, h


def _log_sigmoid(x):
    return -_softplus(-x)


def _mlstm_prompt_kernel(q_ref, k_ref, v_ref, og_ref, gc_ref, gr_ref, g_ref, y_ref, c_out, n_out,
                         m_out, c_ref, n_ref, m_ref, *, L, nc, d, n_hh):
    c_ref[...] = jnp.zeros_like(c_ref)
    n_ref[...] = jnp.zeros_like(n_ref)
    m_ref[...] = jnp.zeros_like(m_ref)
    r = lax.broadcasted_iota(jnp.int32, (L, L), 0)
    cc = lax.broadcasted_iota(jnp.int32, (L, L), 1)
    causal = cc <= r
    neg = jnp.float32(-jnp.inf)

    def chunk_head(start, hh):
        cols = slice(hh * d, (hh + 1) * d)
        q = q_ref[0, pl.ds(start, L), cols]
        ks = k_ref[0, pl.ds(start, L), cols] * (1.0 / math.sqrt(d))
        v = v_ref[0, pl.ds(start, L), cols]
        gcol = gc_ref[0, hh, pl.ds(start, L), :]
        grow = gr_ref[0, hh, :, pl.ds(start, L)]
        i_col, i_row = gcol[:, 0:1], grow[0:1, :]
        lf_col, lf_row = _log_sigmoid(gcol[:, 1:2]), _log_sigmoid(grow[1:2, :])
        b_col = jnp.sum(jnp.where(causal, lf_row, 0.0), axis=-1, keepdims=True)
        b_row = jnp.sum(jnp.where(r <= cc, lf_col, 0.0), axis=0, keepdims=True)
        m_prev = m_ref[hh, 0:1, 0:1]
        dmat = jnp.where(causal, b_col - b_row + i_row, neg)
        m_t = jnp.maximum(b_col + m_prev, jnp.max(dmat, axis=-1, keepdims=True))
        w = jnp.exp(dmat - m_t)
        inter = jnp.exp(b_col + m_prev - m_t)
        s = lax.dot_general(q, ks, NT_DIMS, precision=HI, preferred_element_type=F32)
        sw = w * s
        c = c_ref[hh]
        n = n_ref[hh, 0:1, :]
        num = (jnp.dot(sw, v, precision=HI, preferred_element_type=F32)
               + inter * lax.dot_general(q, c, NT_DIMS, precision=HI, preferred_element_type=F32))
        den = jnp.sum(sw, axis=-1, keepdims=True) + inter * jnp.sum(q * n, axis=-1, keepdims=True)
        h = num / jnp.maximum(jnp.abs(den), jnp.exp(-m_t))
        m_new = m_t[L - 1:L, :]
        decay = inter[L - 1:L, :]
        w_last = jnp.exp(b_col[L - 1:L, :] - b_col + i_col - m_new)
        vw = v * w_last
        c_ref[hh] = decay * c + lax.dot_general(vw, ks, (((0,), (0,)), ((), ())), precision=HI,
                                                preferred_element_type=F32)
        n_ref[hh] = jnp.broadcast_to(decay * n + jnp.sum(ks * w_last, axis=0, keepdims=True),
                                     n_ref.shape[1:])
        m_ref[hh] = jnp.broadcast_to(m_new, m_ref.shape[1:])
        y = h * _sigmoid(og_ref[0, pl.ds(start, L), cols])
        ms = jnp.mean(y * y, axis=-1, keepdims=True)
        y_ref[pl.ds(start, L), cols] = (y * lax.rsqrt(ms + EPS) * g_ref[:, cols]).astype(y_ref.dtype)

    def chunk(ci, _):
        start = pl.multiple_of(ci * L, L)
        for hh in range(n_hh):
            chunk_head(start, hh)
        return 0

    lax.fori_loop(0, nc, chunk, 0)
    c_out[0] = c_ref[...]
    n_out[0] = n_ref[...]
    m_out[0] = m_ref[...]


def mlstm_prompt(z3, gates, gain, B, T, *, L=128):
    H = z3.shape[2] // LANE
    d = LANE
    g4 = gates.reshape(B, T, 2, H)
    gcol = g4.transpose(0, 3, 1, 2)
    grow = g4.transpose(0, 3, 2, 1)
    n_hh = 2 if H % 2 == 0 else 1
    kern = functools.partial(_mlstm_prompt_kernel, L=L, nc=T // L, d=d, n_hh=n_hh)
    col = lambda grp: pl.BlockSpec((1, T, n_hh * d), lambda b, h: (grp, b, h))
    y, c, n, m = pl.pallas_call(
        kern, grid=(B, H // n_hh),
        in_specs=[col(8), col(9), col(10), col(11),
                  pl.BlockSpec((1, n_hh, T, 2), lambda b, h: (b, h, 0, 0)),
                  pl.BlockSpec((1, n_hh, 2, T), lambda b, h: (b, h, 0, 0)),
                  pl.BlockSpec((1, n_hh * d), lambda b, h: (0, h))],
        out_specs=[pl.BlockSpec((T, n_hh * d), lambda b, h: (b, h)),
                   pl.BlockSpec((1, n_hh, d, d), lambda b, h: (b, h, 0, 0)),
                   pl.BlockSpec((1, n_hh, 8, d), lambda b, h: (b, h, 0, 0)),
                   pl.BlockSpec((1, n_hh, 8, LANE), lambda b, h: (b, h, 0, 0))],
        out_shape=[jax.ShapeDtypeStruct((B * T, H * d), BF16),
                   jax.ShapeDtypeStruct((B, H, d, d), F32),
                   jax.ShapeDtypeStruct((B, H, 8, d), F32),
                   jax.ShapeDtypeStruct((B, H, 8, LANE), F32)],
        scratch_shapes=[pltpu.VMEM((n_hh, d, d), F32), pltpu.VMEM((n_hh, 8, d), F32),
                        pltpu.VMEM((n_hh, 8, LANE), F32)],
        compiler_params=_cp("parallel", "parallel"), name="mlstm_prompt")(
            z3, z3, z3, z3, gcol, grow, gain.reshape(1, -1))
    return y, c, n[:, :, 0, :], m[:, :, 0, 0]


def _mlstm_step_kernel(q_ref, k_ref, v_ref, og_ref, i_ref, f_ref, c_ref, n_ref, m_ref, g_ref,
                       y_ref, c_out, n_out, m_out, *, d):
    q, v = q_ref[0], v_ref[0]
    k = k_ref[0] * (1.0 / math.sqrt(d))
    i_pre, lf = i_ref[0], _log_sigmoid(f_ref[0])
    c, n, m = c_ref[0], n_ref[0], m_ref[0]
    m_t = jnp.maximum(lf + m, i_pre)
    w = jnp.exp(i_pre - m_t)
    inter = jnp.exp(lf + m - m_t)
    sw = w * jnp.sum(q * k, axis=-1, keepdims=True)
    eye = (lax.broadcasted_iota(jnp.int32, (1, d, d), 1) == lax.broadcasted_iota(jnp.int32, (1, d, d), 2))
    cq_col = jnp.sum(c * q, axis=-1, keepdims=True)
    cq = jnp.sum(jnp.where(eye, cq_col, 0.0), axis=1, keepdims=True)
    num = sw * v + inter * cq
    den = sw + inter * jnp.sum(n * q, axis=-1, keepdims=True)
    h = num / jnp.maximum(jnp.abs(den), jnp.exp(-m_t))
    v_col = jnp.sum(jnp.where(eye, v, 0.0), axis=-1, keepdims=True)
    c_out[0] = inter * c + (w * v_col) * k
    n_out[0] = inter * n + w * k
    m_out[0] = m_t
    y = h * _sigmoid(og_ref[0])
    ms = jnp.mean(y * y, axis=-1, keepdims=True)
    y_ref[0] = y * lax.rsqrt(ms + EPS) * g_ref[...]


def mlstm_step(q, k, v, og, i_pre, f_pre, c0, n0, m0, gain):
    G, H, d = n0.shape
    vec = lambda a: a.reshape(G, H, 1, d)
    sca = lambda a: a.reshape(G, H, 1, 1)
    vspec = pl.BlockSpec((1, H, 1, d), lambda b: (b, 0, 0, 0))
    sspec = pl.BlockSpec((1, H, 1, 1), lambda b: (b, 0, 0, 0))
    cspec = pl.BlockSpec((1, H, d, d), lambda b: (b, 0, 0, 0))
    kern = functools.partial(_mlstm_step_kernel, d=d)
    y, c, n, m = pl.pallas_call(
        kern, grid=(G,),
        in_specs=[vspec, vspec, vspec, vspec, sspec, sspec, cspec, vspec, sspec,
                  pl.BlockSpec((H, 1, d), lambda b: (0, 0, 0))],
        out_specs=[vspec, cspec, vspec, sspec],
        out_shape=[jax.ShapeDtypeStruct((G, H, 1, d), F32), jax.ShapeDtypeStruct((G, H, d, d), F32),
                   jax.ShapeDtypeStruct((G, H, 1, d), F32), jax.ShapeDtypeStruct((G, H, 1, 1), F32)],
        compiler_params=_cp("parallel"), name="mlstm_step")(
            vec(q), vec(k), vec(v), vec(og), sca(i_pre), sca(f_pre), c0, vec(n0), sca(m0),
            gain.reshape(H, 1, d))
    return y.reshape(G, H * d), c, n.reshape(G, H, d), m.reshape(G, H)


def _page_specs(P, n_pages, block):
    zeros = (0,) * (len(block) - 1)

    def spec(p):
        return pl.BlockSpec(block, lambda b, s, rows: (rows[b * n_pages + s * P + p],) + zeros)
    return [spec(p) for p in range(P)]


def _decode_scores_kernel(rows_ref, q_ref, *rest, P, page):
    pages, o_ref = rest[:P], rest[P]
    n_grp, d = q_ref.shape[1], q_ref.shape[2]
    q = jnp.broadcast_to(q_ref[0] * (1.0 / math.sqrt(d)), (n_grp, d, page))
    for p in range(P):
        o_ref[0, :, :, p * page:(p + 1) * page] = jnp.sum(pages[p][0] * q, axis=1, keepdims=True)


def decode_scores(q, pool_t, rows, n_pages, *, P=32):
    G = q.shape[0]
    _, n_grp, d, page = pool_t.shape
    P = _pick_tile(n_pages, P)
    kern = functools.partial(_decode_scores_kernel, P=P, page=page)
    grid_spec = pltpu.PrefetchScalarGridSpec(
        num_scalar_prefetch=1, grid=(G, n_pages // P),
        in_specs=[pl.BlockSpec((1, n_grp, d, 1), lambda b, s, rows: (b, 0, 0, 0))]
        + _page_specs(P, n_pages, (1, n_grp, d, page)),
        out_specs=pl.BlockSpec((1, n_grp, 1, P * page), lambda b, s, rows: (b, 0, 0, s)))
    return pl.pallas_call(
        kern, grid_spec=grid_spec,
        out_shape=jax.ShapeDtypeStruct((G, n_grp, 1, n_pages * page), F32),
        compiler_params=_cp("parallel", "arbitrary"), name="decode_scores")(
            rows, q.reshape(G, n_grp, d, 1), *([pool_t] * P))


def _sb_weights_kernel(z_ref, w_ref, *, n_tiles):
    bb = z_ref.shape[0]
    r = lax.broadcasted_iota(jnp.int32, (LANE, LANE), 0)
    c = lax.broadcasted_iota(jnp.int32, (LANE, LANE), 1)
    upper = jnp.where(r > c, 1.0, 0.0).astype(F32)
    carry = jnp.zeros((bb * 8, 1), F32)
    for t in reversed(range(n_tiles)):
        z = z_ref[:, :, t * LANE:(t + 1) * LANE].reshape(bb * 8, LANE)
        lk = -_softplus(z)
        la = jnp.dot(lk, upper, precision=HI, preferred_element_type=F32) + carry
        w_ref[:, :, t * LANE:(t + 1) * LANE] = jnp.exp(z + lk + la).reshape(bb, 8, LANE)
        carry = carry + jnp.sum(lk, axis=-1, keepdims=True)


def sb_weights(z, *, bb=8):
    G, _, Tp = z.shape
    bb = _pick_tile(G, bb)
    kern = functools.partial(_sb_weights_kernel, n_tiles=Tp // LANE)
    spec = pl.BlockSpec((bb, 8, Tp), lambda i: (i, 0, 0))
    return pl.pallas_call(kern, grid=(G // bb,), in_specs=[spec], out_specs=spec,
                          out_shape=jax.ShapeDtypeStruct(z.shape, F32),
                          compiler_params=_cp("parallel"), name="sb_weights")(z)


def _diff_weights_kernel(z_ref, q_ref, k_ref, v_ref, lv_ref, w_ref, new_ref, *, lam_init):
    bb, _, Tp = z_ref.shape
    width = q_ref.shape[-1]
    lam = (_diff_lambda(lv_ref) + lam_init).reshape(1, 1, 1)
    prod = q_ref[...] * k_ref[...] * (1.0 / math.sqrt(64.0))
    grp = lax.broadcasted_iota(jnp.int32, (1, 8, width), 2) // 64
    r8 = lax.broadcasted_iota(jnp.int32, (1, 8, width), 1)
    mine = grp == (r8 % 4) * 2 + r8 // 4
    s_new = jnp.sum(jnp.where(mine, prod, 0.0), axis=-1, keepdims=True)
    head = lax.broadcasted_iota(jnp.int32, (1, 8, 1), 1) % 4
    slope = jnp.exp((-2.0 * math.log(2.0)) * (head + 1).astype(F32))
    kpos = lax.broadcasted_iota(jnp.int32, (1, 1, Tp), 2)
    s = z_ref[...] - slope * (Tp - kpos).astype(F32)
    m = jnp.maximum(jnp.max(s, axis=-1, keepdims=True), s_new)
    p = jnp.exp(s - m)
    p_new = jnp.exp(s_new - m)
    den = jnp.sum(p, axis=-1, keepdims=True) + p_new
    p = p / den
    p_new = p_new / den
    w = p[:, 0:4, :] - lam * p[:, 4:8, :]
    w8 = jnp.concatenate([w, jnp.zeros_like(w)], axis=1).reshape(bb * 8, Tp)
    n_heads = 4
    spread = (lax.broadcasted_iota(jnp.int32, (LANE, n_heads * LANE), 1) // n_heads
              == lax.broadcasted_iota(jnp.int32, (LANE, n_heads * LANE), 0)).astype(F32)
    keep = (lax.broadcasted_iota(jnp.int32, (bb * 8, n_heads * LANE), 1) % n_heads
            == lax.broadcasted_iota(jnp.int32, (bb * 8, n_heads * LANE), 0) % 8)
    for t in range(Tp // LANE):
        wide = jnp.dot(w8[:, t * LANE:(t + 1) * LANE], spread, precision=HI,
                       preferred_element_type=F32)
        w_ref[:, :, t * n_heads * LANE:(t + 1) * n_heads * LANE] = jnp.where(keep, wide, 0.0).reshape(
            bb, 8, n_heads * LANE)
    w_new = p_new[:, 0:4, :] - lam * p_new[:, 4:8, :]
    new_ref[...] = w_new * v_ref[...]


def diff_weights(z, q, k_new, v_new, lam_vec, lam_init, *, bb=8):
    G, _, Tp = z.shape
    width = q.shape[-1]
    bb = _pick_tile(G, bb)
    kern = functools.partial(_diff_weights_kernel, lam_init=lam_init)
    zspec = pl.BlockSpec((bb, 8, Tp), lambda i: (i, 0, 0))
    qspec = pl.BlockSpec((bb, 1, width), lambda i: (i, 0, 0))
    vspec = pl.BlockSpec((bb, 4, LANE), lambda i: (i, 0, 0))
    r3 = lambda a: a.reshape(G, 1, width)
    return pl.pallas_call(
        kern, grid=(G // bb,),
        in_specs=[zspec, qspec, qspec, vspec, pl.BlockSpec(lam_vec.shape, lambda i: (0, 0))],
        out_specs=[pl.BlockSpec((bb, 8, 4 * Tp), lambda i: (i, 0, 0)), vspec],
        out_shape=[jax.ShapeDtypeStruct((G, 8, 4 * Tp), F32), jax.ShapeDtypeStruct((G, 4, LANE), F32)],
        compiler_params=_cp("parallel"), name="diff_weights")(
            z, r3(q), r3(k_new), v_new.reshape(G, 4, LANE), lam_vec)


def _decode_pv_t_kernel(rows_ref, w_ref, *rest, P, page):
    pages, o_ref, acc_ref = rest[:P], rest[P], rest[P + 1]
    s = pl.program_id(1)

    @pl.when(s == 0)
    def _():
        acc_ref[...] = jnp.zeros_like(acc_ref)

    acc = acc_ref[...]
    for p in range(P):
        acc = acc + pages[p][0] * w_ref[0, :, :, p * page:(p + 1) * page]
    acc_ref[...] = acc

    @pl.when(s == pl.num_programs(1) - 1)
    def _():
        d = acc.shape[1]
        col = jnp.sum(acc, axis=-1, keepdims=True)
        eye = (lax.broadcasted_iota(jnp.int32, (1, d, d), 1)
               == lax.broadcasted_iota(jnp.int32, (1, d, d), 2))
        o_ref[0] = jnp.sum(jnp.where(eye, col, 0.0), axis=1, keepdims=True)


def decode_pv_t(w, pool_t, rows, n_pages, *, P=32):
    G = w.shape[0]
    _, n_grp, d, page = pool_t.shape
    P = _pick_tile(n_pages, P)
    kern = functools.partial(_decode_pv_t_kernel, P=P, page=page)
    grid_spec = pltpu.PrefetchScalarGridSpec(
        num_scalar_prefetch=1, grid=(G, n_pages // P),
        in_specs=[pl.BlockSpec((1, n_grp, 1, P * page), lambda b, s, rows: (b, 0, 0, s))]
        + _page_specs(P, n_pages, (1, n_grp, d, page)),
        out_specs=pl.BlockSpec((1, n_grp, 1, d), lambda b, s, rows: (b, 0, 0, 0)),
        scratch_shapes=[pltpu.VMEM((n_grp, d, page), F32)])
    out = pl.pallas_call(
        kern, grid_spec=grid_spec, out_shape=jax.ShapeDtypeStruct((G, n_grp, 1, d), F32),
        compiler_params=_cp("parallel", "arbitrary"), name="decode_pv_t")(rows, w, *([pool_t] * P))
    return out.reshape(G, n_grp * d)


def _decode_pv_rows_kernel(rows_ref, w_ref, init_ref, *rest, P, n_rows):
    pages, o_ref, acc_ref = rest[:P], rest[P], rest[P + 1]
    s = pl.program_id(1)

    @pl.when(s == 0)
    def _():
        acc_ref[...] = jnp.zeros_like(acc_ref)

    acc = acc_ref[...]
    for p in range(P):
        acc = acc + _dot3(w_ref[0, :, p * n_rows:(p + 1) * n_rows], pages[p][0])
    acc_ref[...] = acc

    @pl.when(s == pl.num_programs(1) - 1)
    def _():
        o_ref[0] = acc[0:4] + init_ref[0]


def decode_pv_rows(w, pool, rows, init, n_pages, *, P=32):
    G = w.shape[0]
    _, n_rows, width = pool.shape
    P = _pick_tile(n_pages, P)
    kern = functools.partial(_decode_pv_rows_kernel, P=P, n_rows=n_rows)
    grid_spec = pltpu.PrefetchScalarGridSpec(
        num_scalar_prefetch=1, grid=(G, n_pages // P),
        in_specs=[pl.BlockSpec((1, 8, P * n_rows), lambda b, s, rows: (b, 0, s)),
                  pl.BlockSpec((1, 4, width), lambda b, s, rows: (b, 0, 0))]
        + _page_specs(P, n_pages, (1, n_rows, width)),
        out_specs=pl.BlockSpec((1, 4, width), lambda b, s, rows: (b, 0, 0)),
        scratch_shapes=[pltpu.VMEM((8, width), F32)])
    out = pl.pallas_call(
        kern, grid_spec=grid_spec, out_shape=jax.ShapeDtypeStruct((G, 4, width), F32),
        compiler_params=_cp("parallel", "arbitrary"), name="decode_pv_rows")(
            rows, w, init, *([pool] * P))
    return out.reshape(G, 4 * width)


def _group_norm_kernel(x_ref, g_ref, o_ref, *, group, post_scale):
    x = x_ref[...]
    ms = jnp.dot(x * x, _group_mean_matrix(x.shape[-1], group), precision=HI,
                 preferred_element_type=F32)
    o_ref[...] = x * lax.rsqrt(ms + EPS) * g_ref[...] * post_scale


def _moe_up_kernel(te_ref, act_ref, a_ref, wg_ref, wu_ref, o_ref):
    i = pl.program_id(1)

    @pl.when(act_ref[i] > 0)
    def _():
        a = a_ref[...].astype(BF16)
        g = jnp.dot(a, wg_ref[...], preferred_element_type=F32)
        u = jnp.dot(a, wu_ref[...], preferred_element_type=F32)
        o_ref[...] = (g * _sigmoid(g) * u).astype(o_ref.dtype)

    @pl.when(act_ref[i] == 0)
    def _():
        o_ref[...] = jnp.zeros_like(o_ref)


def _moe_down_kernel(te_ref, act_ref, a_ref, wd_ref, o_ref):
    i = pl.program_id(1)

    @pl.when(act_ref[i] > 0)
    def _():
        o_ref[...] = jnp.dot(a_ref[...], wd_ref[...], preferred_element_type=F32)

    @pl.when(act_ref[i] == 0)
    def _():
        o_ref[...] = jnp.zeros_like(o_ref)


def moe_grouped(xg, tile_expert, tile_active, wg, wu, wd, *, tg, tn_up=1408, tn_down=1024):
    n_rows, D = xg.shape
    F = wg.shape[-1]
    n_tiles = n_rows // tg
    tn_up, tn_down = _pick_tile(F, tn_up), _pick_tile(D, tn_down)
    up_spec = pltpu.PrefetchScalarGridSpec(
        num_scalar_prefetch=2, grid=(F // tn_up, n_tiles),
        in_specs=[pl.BlockSpec((tg, D), lambda j, i, te, ac: (i, 0)),
                  pl.BlockSpec((None, D, tn_up), lambda j, i, te, ac: (te[i], 0, j)),
                  pl.BlockSpec((None, D, tn_up), lambda j, i, te, ac: (te[i], 0, j))],
        out_specs=pl.BlockSpec((tg, tn_up), lambda j, i, te, ac: (i, j)))
    act = pl.pallas_call(_moe_up_kernel, grid_spec=up_spec,
                         out_shape=jax.ShapeDtypeStruct((n_rows, F), BF16),
                         compiler_params=_cp("parallel", "arbitrary"), name="moe_up")(
                             tile_expert, tile_active, xg, wg, wu)
    down_spec = pltpu.PrefetchScalarGridSpec(
        num_scalar_prefetch=2, grid=(D // tn_down, n_tiles),
        in_specs=[pl.BlockSpec((tg, F), lambda j, i, te, ac: (i, 0)),
                  pl.BlockSpec((None, F, tn_down), lambda j, i, te, ac: (te[i], 0, j))],
        out_specs=pl.BlockSpec((tg, tn_down), lambda j, i, te, ac: (i, j)))
    return pl.pallas_call(_moe_down_kernel, grid_spec=down_spec,
                          out_shape=jax.ShapeDtypeStruct((n_rows, D), F32),
                          compiler_params=_cp("parallel", "arbitrary"), name="moe_down")(
                              tile_expert, tile_active, act, wd)


def _moe_dispatch(idx, n_experts, tg):
    M = idx.shape[0]
    flat_e = idx.reshape(-1)
    onehot = (flat_e[:, None] == jnp.arange(n_experts, dtype=jnp.int32)[None, :]).astype(jnp.int32)
    counts = jnp.sum(onehot, axis=0)
    rank = jnp.sum((jnp.cumsum(onehot, axis=0) - onehot) * onehot, axis=1)
    padded = (counts + tg - 1) // tg * tg
    ends = jnp.cumsum(padded)
    pos = (ends - padded)[flat_e] + rank
    n_rows = 2 * M + n_experts * tg
    row_token = jnp.zeros((n_rows,), jnp.int32).at[pos].set(jnp.arange(2 * M, dtype=jnp.int32) // 2)
    tile_start = jnp.arange(n_rows // tg, dtype=jnp.int32) * tg
    tile_expert = jnp.minimum(jnp.sum((tile_start[:, None] >= ends[None, :]).astype(jnp.int32), axis=1),
                              n_experts - 1)
    tile_active = (tile_start < ends[-1]).astype(jnp.int32)
    return pos.reshape(M, 2), row_token, tile_expert.astype(jnp.int32), tile_active


def _moe_combine_kernel(x_ref, g_ref, y1_ref, y2_ref, w_ref, o_ref):
    w = w_ref[...]
    f = w[:, 0:1] * y1_ref[...] + w[:, 1:2] * y2_ref[...]
    o_ref[...] = x_ref[...] + g_ref[0] * f


def moe_combine(x, g, y1, y2, wts, rows_per_gate, *, tm=512):
    M, D = x.shape
    tm = _pick_tile(rows_per_gate, tm)
    rpt = rows_per_gate // tm
    row = pl.BlockSpec((tm, D), lambda i: (i, 0))
    return pl.pallas_call(
        _moe_combine_kernel, grid=(M // tm,),
        in_specs=[row, pl.BlockSpec((1, 1, D), lambda i: (i // rpt, 0, 0)), row, row,
                  pl.BlockSpec((tm, LANE), lambda i: (i, 0))],
        out_specs=row, out_shape=jax.ShapeDtypeStruct((M, D), F32),
        compiler_params=_cp("parallel"), name="moe_combine")(x, g.reshape(-1, 1, D), y1, y2, wts)


def _moe_dense_up_kernel(a_ref, wg_ref, wu_ref, o_ref, accg_ref, accu_ref, *, nk):
    k = pl.program_id(2)
    a = a_ref[...]
    pg = _dot3(a, wg_ref[...])
    pu = _dot3(a, wu_ref[...])

    @pl.when(k == 0)
    def _():
        accg_ref[...] = pg
        accu_ref[...] = pu

    @pl.when(k > 0)
    def _():
        accg_ref[...] += pg
        accu_ref[...] += pu

    @pl.when(k == nk - 1)
    def _():
        g = accg_ref[...]
        o_ref[...] = g * _sigmoid(g) * accu_ref[...]


def _moe_dense_down_kernel(a_ref, wd_ref, cw_ref, x_ref, g_ref, o_ref, acc_ref):
    e, k = pl.program_id(0), pl.program_id(1)
    part = cw_ref[...] * _dot3(a_ref[...], wd_ref[...])

    @pl.when((e == 0) & (k == 0))
    def _():
        acc_ref[...] = part

    @pl.when((e > 0) | (k > 0))
    def _():
        acc_ref[...] += part

    @pl.when((e == pl.num_programs(0) - 1) & (k == pl.num_programs(1) - 1))
    def _():
        o_ref[...] = x_ref[...] + g_ref[...] * acc_ref[...]


def moe_dense(h, x, gate, combine, wg, wu, wd, *, tn=1408, tk=1024, tkd=1408):
    M, D = h.shape
    E, _, F = wg.shape
    tn, tk, tkd = _pick_tile(F, tn), _pick_tile(D, tk), _pick_tile(F, tkd)
    nk = D // tk
    wspec = pl.BlockSpec((None, tk, tn), lambda e, j, k: (e, k, j))
    act = pl.pallas_call(
        functools.partial(_moe_dense_up_kernel, nk=nk), grid=(E, F // tn, nk),
        in_specs=[pl.BlockSpec((M, tk), lambda e, j, k: (0, k)), wspec, wspec],
        out_specs=pl.BlockSpec((None, M, tn), lambda e, j, k: (e, 0, j)),
        out_shape=jax.ShapeDtypeStruct((E, M, F), F32),
        scratch_shapes=[pltpu.VMEM((M, tn), F32), pltpu.VMEM((M, tn), F32)],
        compiler_params=_cp("parallel", "parallel", "arbitrary"), name="moe_dense_up")(h, wg, wu)
    full = pl.BlockSpec((M, D), lambda e, k: (0, 0))
    return pl.pallas_call(
        _moe_dense_down_kernel, grid=(E, F // tkd),
        in_specs=[pl.BlockSpec((None, M, tkd), lambda e, k: (e, 0, k)),
                  pl.BlockSpec((None, tkd, D), lambda e, k: (e, k, 0)),
                  pl.BlockSpec((None, M, 1), lambda e, k: (e, 0, 0)), full, full],
        out_specs=full, out_shape=jax.ShapeDtypeStruct((M, D), F32),
        scratch_shapes=[pltpu.VMEM((M, D), F32)],
        compiler_params=_cp("arbitrary", "arbitrary"), name="moe_dense_down")(
            act, wd, combine, x, gate)


def group_norm(x, gain, *, group, post_scale=1.0):
    kern = functools.partial(_group_norm_kernel, group=group, post_scale=post_scale)
    return pl.pallas_call(kern, out_shape=jax.ShapeDtypeStruct(x.shape, F32), name="group_norm")(
        x, gain.reshape(1, -1))


W_GROUP = 512
N_MAIN = 12 * W_GROUP
MOE_TILE = 256


def _lam_init(l):
    return 0.8 - 0.6 * math.exp(-0.3 * l)


def _gate_weight(w_in_l):
    wg = w_in_l[:, N_MAIN:]
    return jnp.zeros((wg.shape[0], LANE), F32).at[:, :wg.shape[1]].set(wg)


def _prompt_trunk(x3, mods, p, wb):
    B, T, D = x3.shape
    M = B * T
    x = x3.reshape(M, D)
    news = []
    for l in range(len(mods)):
        sh1, sc1, g1, sh2, sc2, g2 = mods[l]
        h = norm_mod(x, p['g_norm1'][l], sc1, sh1, out_dtype=BF16, rows_per_mod=T, name="norm1_p")
        z3 = matmul(h, wb['w_in'], precise=False, layer=l, n_cols=N_MAIN, out3d_width=W_GROUP,
                    tm=1024, tn=W_GROUP, name="w_in_p")
        gates = matmul(h, _gate_weight(p['w_in'][l]).astype(BF16), precise=False, name="gates_p")
        gates = gates[:, :8] + jnp.concatenate([p['ml_b_i'][l], p['ml_b_f'][l]])[None, :]
        y_sb = sb_attention_prompt(z3, p['g_sb_out'][l], B, T)
        y_df = diff_attention_prompt(z3, p['g_diff_out'][l], p['diff_lam'][l], _lam_init(l), B, T)
        y_lru, conv_new, h_new = lru_prompt(z3, p, l, B, T)
        y_ml, c_new, n_new, m_new = mlstm_prompt(z3, gates, p['g_ml_out'][l], B, T)
        ycat = jnp.concatenate([y_sb, y_df, y_lru, y_ml], axis=-1)
        x = matmul(ycat, wb['w_out'], precise=False, layer=l, resid=(x, g1), rows_per_gate=T,
                   name="w_out_p")
        j = l // 2
        if l % 2 == 0:
            h2 = norm_mod(x, p['g_norm2'][l], sc2, sh2, out_dtype=BF16, rows_per_mod=T, name="norm2_p")
            act = swiglu_up(h2, wb['ffn_w_gate'], wb['ffn_w_up'], precise=False, out_dtype=BF16,
                            layer=j, name="ffn_up_p")
            x = matmul(act, wb['ffn_w_down'], precise=False, layer=j, tk=1408, resid=(x, g2),
                       rows_per_gate=T, name="ffn_down_p")
        else:
            h2, idx, wts = norm_mod(x, p['g_norm2'][l], sc2, sh2, out_dtype=F32, rows_per_mod=T,
                                    router=(p['moe_w_router'][j], p['moe_b_router'][j]),
                                    name="norm2_router_p")
            n_experts = p['moe_w_router'].shape[-1]
            pos, row_token, tile_expert, tile_active = _moe_dispatch(idx[:, :2], n_experts, MOE_TILE)
            take = lambda a, rows: a.at[rows].get(mode="promise_in_bounds")
            xg = take(h2, row_token)
            yg = moe_grouped(xg, tile_expert, tile_active, wb['moe_w_gate'][j], wb['moe_w_up'][j],
                             wb['moe_w_down'][j], tg=MOE_TILE)
            x = moe_combine(x, g2, take(yg, pos[:, 0]), take(yg, pos[:, 1]), wts, T)
        news.append((z3[1].reshape(B, T, 8, 64), z3[2].reshape(B, T, 8, 64),
                     z3[4].reshape(B, T, 4, 2, 64), z3[5].reshape(B, T, 4, 128),
                     conv_new, h_new, c_new, n_new, m_new))
    y = norm_mod(x, p['g_final'], out_dtype=F32, name="final_norm_p").reshape(B, T, D)
    return y, tuple(jnp.stack([nw[i] for nw in news], axis=0) for i in range(9))


def _sample_trunk(x3, mods, p, caches, states, page_table):
    G, _, D = x3.shape
    x = x3.reshape(G, D)
    sb_k_c, sb_v_c, df_k_c, df_v_c = caches
    n_layers, n_phys, page = sb_k_c.shape[:3]
    n_pool = n_layers * n_phys
    pool_sb_k = sb_k_c.transpose(0, 1, 3, 4, 2).reshape(n_pool, 8, 64, page)
    pool_sb_v = sb_v_c.transpose(0, 1, 3, 4, 2).reshape(n_pool, 8, 64, page)
    pool_df_k = df_k_c.transpose(0, 1, 3, 4, 5, 2).reshape(n_pool, 8, 64, page)
    pool_df_v = df_v_c.reshape(n_pool, page * 4, LANE)
    n_pages = page_table.shape[1]
    conv_all, h_all, c_all, n_all, m_all = states
    news = []
    for l in range(len(mods)):
        sh1, sc1, g1, sh2, sc2, g2 = mods[l]
        rows = (page_table + l * n_phys).reshape(-1).astype(jnp.int32)
        h = norm_mod(x, p['g_norm1'][l], sc1, sh1, out_dtype=F32, name="norm1_s")
        z = matmul(h, p['w_in'].swapaxes(1, 2), precise=True, layer=l, n_cols=N_MAIN, tn=W_GROUP,
                   b_transposed=True, name="w_in_s")
        gates = matmul(h, _gate_weight(p['w_in'][l]), precise=True, name="gates_s")
        zs = [z[:, i * W_GROUP:(i + 1) * W_GROUP] for i in range(12)]
        sb_q, sb_k, sb_v, df_q, df_k, df_v, lru_x, lru_g, ml_q, ml_k, ml_v, ml_o = zs
        ml_i = gates[:, 0:4] + p['ml_b_i'][l][None, :]
        ml_f = gates[:, 4:8] + p['ml_b_f'][l][None, :]
        Tp = n_pages * page
        z_sb = decode_scores(sb_q, pool_sb_k, rows, n_pages).reshape(G, 8, Tp)
        w_sb = sb_weights(z_sb).reshape(G, 8, 1, Tp)
        y_sb = decode_pv_t(w_sb, pool_sb_v, rows, n_pages)
        y_sb = group_norm(y_sb, p['g_sb_out'][l], group=64)
        z_df = decode_scores(df_q, pool_df_k, rows, n_pages)
        z_df = z_df.reshape(G, 4, 2, Tp).swapaxes(1, 2).reshape(G, 8, Tp)
        w_df, new_df = diff_weights(z_df, df_q, df_k, df_v, p['diff_lam'][l], _lam_init(l))
        y_df = decode_pv_rows(w_df, pool_df_v, rows, new_df, n_pages)
        y_df = group_norm(y_df, p['g_diff_out'][l], group=128, post_scale=1.0 - _lam_init(l))
        y_lru, conv_new, h_new = lru_step(lru_x, lru_g, conv_all[l], h_all[l], p, l)
        y_ml, c_new, n_new, m_new = mlstm_step(ml_q, ml_k, ml_v, ml_o, ml_i, ml_f, c_all[l],
                                               n_all[l], m_all[l], p['g_ml_out'][l])
        ycat = jnp.concatenate([y_sb, y_df, y_lru, y_ml], axis=-1)
        x = matmul(ycat, p['w_out'], precise=True, layer=l, resid=(x, g1), name="w_out_s")
        j = l // 2
        if l % 2 == 0:
            h2 = norm_mod(x, p['g_norm2'][l], sc2, sh2, out_dtype=F32, name="norm2_s")
            act = swiglu_up(h2, p['ffn_w_gate'], p['ffn_w_up'], precise=True, out_dtype=F32,
                            layer=j, tk=1024, name="ffn_up_s")
            x = matmul(act, p['ffn_w_down'], precise=True, layer=j, tk=1408, resid=(x, g2),
                       name="ffn_down_s")
        else:
            h2, idx, wts = norm_mod(x, p['g_norm2'][l], sc2, sh2, out_dtype=F32,
                                    router=(p['moe_w_router'][j], p['moe_b_router'][j]),
                                    name="norm2_router_s")
            n_experts = p['moe_w_router'].shape[-1]
            e_ids = jnp.arange(n_experts, dtype=jnp.int32)[:, None]
            combine = (jnp.where(idx[None, :, 0] == e_ids, wts[None, :, 0], 0.0)
                       + jnp.where(idx[None, :, 1] == e_ids, wts[None, :, 1], 0.0))[..., None]
            x = moe_dense(h2, x, g2, combine, p['moe_w_gate'][j], p['moe_w_up'][j], p['moe_w_down'][j])
        news.append((sb_k.reshape(G, 1, 8, 64), sb_v.reshape(G, 1, 8, 64),
                     df_k.reshape(G, 1, 4, 2, 64), df_v.reshape(G, 1, 4, 128),
                     conv_new, h_new, c_new, n_new, m_new))
    y = norm_mod(x, p['g_final'], out_dtype=F32, name="final_norm_s").reshape(G, 1, D)
    return y, tuple(jnp.stack([nw[i] for nw in news], axis=0) for i in range(9))


def kernel(x_prompt, x_sample, cache_sb_k, cache_sb_v, cache_diff_k, cache_diff_v,
           state_lru_conv, state_lru_h, state_mlstm_c, state_mlstm_n, state_mlstm_m,
           page_table, c_prompt, c_sample, w_ada, b_ada, g_norm1, g_norm2, w_in, w_out,
           g_sb_out, diff_lam, g_diff_out, lru_conv_w, lru_conv_b, lru_w_r, lru_b_r,
           lru_w_i, lru_b_i, lru_lam, g_lru_out, ml_b_i, ml_b_f, g_ml_out,
           ffn_w_gate, ffn_w_up, ffn_w_down, moe_w_router, moe_b_router,
           moe_w_gate, moe_w_up, moe_w_down, g_final):
    p = dict(w_ada=w_ada, b_ada=b_ada, g_norm1=g_norm1, g_norm2=g_norm2, w_in=w_in, w_out=w_out,
             g_sb_out=g_sb_out, diff_lam=diff_lam, g_diff_out=g_diff_out, lru_conv_w=lru_conv_w,
             lru_conv_b=lru_conv_b, lru_w_r=lru_w_r, lru_b_r=lru_b_r, lru_w_i=lru_w_i,
             lru_b_i=lru_b_i, lru_lam=lru_lam, g_lru_out=g_lru_out, ml_b_i=ml_b_i, ml_b_f=ml_b_f,
             g_ml_out=g_ml_out, ffn_w_gate=ffn_w_gate, ffn_w_up=ffn_w_up, ffn_w_down=ffn_w_down,
             moe_w_router=moe_w_router, moe_b_router=moe_b_router, moe_w_gate=moe_w_gate,
             moe_w_up=moe_w_up, moe_w_down=moe_w_down, g_final=g_final)
    depth, D = g_norm1.shape
    Bp, Gs = x_prompt.shape[0], x_sample.shape[0]
    n_c = Bp + Gs
    c_all = jnp.zeros(((n_c + 15) // 16 * 16, D), F32).at[:Bp].set(c_prompt).at[Bp:n_c].set(c_sample)
    mods_p, mods_s = [], []
    for l in range(depth):
        m = matmul(c_all, w_ada, precise=True, layer=l, bias=b_ada[l], silu_a=True, name="ada")
        mods_p.append([m[:Bp, i * D:(i + 1) * D] for i in range(6)])
        mods_s.append([m[Bp:n_c, i * D:(i + 1) * D] for i in range(6)])
    wb = {k: p[k].astype(BF16) for k in ('w_in', 'w_out', 'ffn_w_gate', 'ffn_w_up', 'ffn_w_down',
                                         'moe_w_gate', 'moe_w_up', 'moe_w_down')}
    y_p, new_p = _prompt_trunk(x_prompt, mods_p, p, wb)
    y_s, new_s = _sample_trunk(x_sample, mods_s, p,
                               (cache_sb_k, cache_sb_v, cache_diff_k, cache_diff_v),
                               (state_lru_conv, state_lru_h, state_mlstm_c, state_mlstm_n,
                                state_mlstm_m), page_table)
    return (y_p, y_s) + new_p + new_s
```

```python
import functools
import math

import jax
import jax.numpy as jnp
from jax import lax
from jax.experimental import pallas as pl
from jax.experimental.pallas import tpu as pltpu

F32 = jnp.float32
BF16 = jnp.bfloat16
HI = lax.Precision.HIGHEST
EPS = 1e-6
LRU_C = 8.0
LANE = 128
VMEM_LIMIT = 56 * 1024 * 1024

NT_DIMS = (((1,), (1,)), ((), ()))


def _cp(*sem):
    return pltpu.CompilerParams(dimension_semantics=sem, vmem_limit_bytes=VMEM_LIMIT)


def _split(x):
    hi = x.astype(BF16)
    return hi, (x - hi.astype(F32)).astype(BF16)


def _dot3(a, b, dims=(((1,), (0,)), ((), ()))):
    m = a.shape[0]
    a_hi = a.astype(BF16)
    a_parts = jnp.concatenate([a, a - a_hi.astype(F32)], axis=0).astype(BF16)
    b_hi, b_lo = _split(b)
    r = lax.dot_general(a_parts, b_hi, dims, preferred_element_type=F32)
    return r[:m] + r[m:] + lax.dot_general(a_hi, b_lo, dims, preferred_element_type=F32)


def _dot(a, b, precise):
    if precise:
        return _dot3(a, b)
    return jnp.dot(a.astype(BF16), b.astype(BF16), preferred_element_type=F32)


def _softplus(x):
    return jnp.maximum(x, 0.0) + jnp.log1p(jnp.exp(-jnp.abs(x)))


def _sigmoid(x):
    return 1.0 / (1.0 + jnp.exp(-x))


def _group_mean_matrix(width, group):
    r = lax.broadcasted_iota(jnp.int32, (width, width), 0) // group
    c = lax.broadcasted_iota(jnp.int32, (width, width), 1) // group
    return jnp.where(r == c, 1.0 / group, 0.0).astype(F32)


def _pick_tile(n, pref):
    t = min(n, pref)
    while n % t:
        t //= 2
    return t


def _mm_kernel(*refs, nk, precise, has_bias, silu_a, resid, b_transposed):
    a_ref, b_ref = refs[0], refs[1]
    i = 2
    bias_ref = x_ref = g_ref = None
    if has_bias:
        bias_ref = refs[i]
        i += 1
    if resid:
        x_ref, g_ref = refs[i], refs[i + 1]
        i += 2
    o_ref, acc_ref = refs[i], refs[i + 1]
    k = pl.program_id(2)
    a = a_ref[...]
    if silu_a:
        a = a * _sigmoid(a)
    if b_transposed:
        assert precise
        part = _dot3(a, b_ref[...], NT_DIMS)
    else:
        part = _dot(a, b_ref[...], precise)

    def finish(acc):
        if has_bias:
            acc = acc + bias_ref[...]
        if resid:
            g = g_ref[...].reshape(-1, acc.shape[-1])
            acc = x_ref[...] + g * acc
        o_ref[...] = acc.reshape(o_ref.shape).astype(o_ref.dtype)

    if nk == 1:
        finish(part)
    else:
        @pl.when(k == 0)
        def _():
            acc_ref[...] = part

        @pl.when(k > 0)
        def _():
            acc_ref[...] += part

        @pl.when(k == nk - 1)
        def _():
            finish(acc_ref[...])


def _weight_spec(b, tk, tn, layer, transposed=False):
    if transposed:
        return pl.BlockSpec((None, tn, tk), lambda i, j, k: (layer, j, k))
    if b.ndim == 2:
        return pl.BlockSpec((tk, tn), lambda i, j, k: (k, j))
    return pl.BlockSpec((None, tk, tn), lambda i, j, k: (layer, k, j))


def matmul(a, b, *, precise, out_dtype=F32, tm=1024, tn=1024, tk=2048, bias=None, silu_a=False,
           out3d_width=None, resid=None, rows_per_gate=None, layer=0, n_cols=None,
           b_transposed=False, name="mm"):
    M, K = a.shape
    N = n_cols if n_cols is not None else (b.shape[-2] if b_transposed else b.shape[-1])
    tm, tn, tk = _pick_tile(M, tm), _pick_tile(N, tn), _pick_tile(K, tk)
    if rows_per_gate is not None:
        tm = _pick_tile(rows_per_gate, tm)
    nk = K // tk
    in_specs = [pl.BlockSpec((tm, tk), lambda i, j, k: (i, k)),
                _weight_spec(b, tk, tn, layer, b_transposed)]
    args = [a, b]
    if bias is not None:
        in_specs.append(pl.BlockSpec((1, tn), lambda i, j, k: (0, j)))
        args.append(bias.reshape(1, N))
    if resid is not None:
        x, g = resid
        in_specs.append(pl.BlockSpec((tm, tn), lambda i, j, k: (i, j)))
        args.append(x)
        if rows_per_gate is None:
            in_specs.append(pl.BlockSpec((tm, tn), lambda i, j, k: (i, j)))
            args.append(g)
        else:
            assert rows_per_gate % tm == 0
            rpt = rows_per_gate // tm
            in_specs.append(pl.BlockSpec((1, 1, tn), lambda i, j, k: (i // rpt, 0, j)))
            args.append(g.reshape(g.shape[0], 1, N))
    if out3d_width is None:
        out_shape = jax.ShapeDtypeStruct((M, N), out_dtype)
        out_spec = pl.BlockSpec((tm, tn), lambda i, j, k: (i, j))
    else:
        assert tn == out3d_width
        out_shape = jax.ShapeDtypeStruct((N // tn, M, tn), out_dtype)
        out_spec = pl.BlockSpec((1, tm, tn), lambda i, j, k: (j, i, 0))
    kern = functools.partial(_mm_kernel, nk=nk, precise=precise, has_bias=bias is not None,
                             silu_a=silu_a, resid=resid is not None, b_transposed=b_transposed)
    return pl.pallas_call(
        kern, grid=(M // tm, N // tn, nk), in_specs=in_specs, out_specs=out_spec,
        out_shape=out_shape, scratch_shapes=[pltpu.VMEM((tm, tn), F32)],
        compiler_params=_cp("parallel", "parallel", "arbitrary"), name=name)(*args)


def _w_in_prompt_kernel(a_ref, b_ref, z_ref, *kt_refs, kt_groups):
    res = jnp.dot(a_ref[...], b_ref[...], preferred_element_type=F32)
    z_ref[0] = res
    j = pl.program_id(1)
    for grp, ref in zip(kt_groups, kt_refs):
        @pl.when(j == grp)
        def _(ref=ref):
            ref[0] = res.T


def w_in_prompt(h, w, layer, B, T, *, n_groups, width, kt_groups, tm=1024):
    M, K = h.shape
    tm = _pick_tile(T, tm)
    per_seq = T // tm
    kern = functools.partial(_w_in_prompt_kernel, kt_groups=kt_groups)
    kt_spec = pl.BlockSpec((1, width, tm), lambda i, j: (i // per_seq, 0, i % per_seq))
    return pl.pallas_call(
        kern, grid=(M // tm, n_groups),
        in_specs=[pl.BlockSpec((tm, K), lambda i, j: (i, 0)),
                  pl.BlockSpec((None, K, width), lambda i, j: (layer, 0, j))],
        out_specs=[pl.BlockSpec((1, tm, width), lambda i, j: (j, i, 0))] + [kt_spec] * len(kt_groups),
        out_shape=[jax.ShapeDtypeStruct((n_groups, M, width), F32)]
        + [jax.ShapeDtypeStruct((B, width, T), F32)] * len(kt_groups),
        compiler_params=_cp("parallel", "arbitrary"), name="w_in_p")(h, w)


def _swiglu_kernel(a_ref, wg_ref, wu_ref, o_ref, accg_ref, accu_ref, *, nk, precise):
    k = pl.program_id(2)
    a = a_ref[...]
    pg = _dot(a, wg_ref[...], precise)
    pu = _dot(a, wu_ref[...], precise)

    def finish(g, u):
        o_ref[...] = (g * _sigmoid(g) * u).astype(o_ref.dtype)

    if nk == 1:
        finish(pg, pu)
    else:
        @pl.when(k == 0)
        def _():
            accg_ref[...] = pg
            accu_ref[...] = pu

        @pl.when(k > 0)
        def _():
            accg_ref[...] += pg
            accu_ref[...] += pu

        @pl.when(k == nk - 1)
        def _():
            finish(accg_ref[...], accu_ref[...])


def swiglu_up(a, wg, wu, *, precise, out_dtype, tm=1024, tn=512, tk=2048, layer=0, name="swiglu_up"):
    M, K = a.shape
    N = wg.shape[-1]
    tm, tn, tk = _pick_tile(M, tm), _pick_tile(N, tn), _pick_tile(K, tk)
    nk = K // tk
    kern = functools.partial(_swiglu_kernel, nk=nk, precise=precise)
    return pl.pallas_call(
        kern, grid=(M // tm, N // tn, nk),
        in_specs=[pl.BlockSpec((tm, tk), lambda i, j, k: (i, k)),
                  _weight_spec(wg, tk, tn, layer), _weight_spec(wu, tk, tn, layer)],
        out_specs=pl.BlockSpec((tm, tn), lambda i, j, k: (i, j)),
        out_shape=jax.ShapeDtypeStruct((M, N), out_dtype),
        scratch_shapes=[pltpu.VMEM((tm, tn), F32), pltpu.VMEM((tm, tn), F32)],
        compiler_params=_cp("parallel", "parallel", "arbitrary"), name=name)(a, wg, wu)


def _norm_kernel(*refs, modulated, router, n_experts):
    x_ref, g_ref = refs[0], refs[1]
    i = 2
    if modulated:
        sc_ref, sh_ref = refs[i], refs[i + 1]
        i += 2
    if router:
        wr_ref, br_ref = refs[i], refs[i + 1]
        i += 2
    o_ref = refs[i]
    x = x_ref[...]
    d = x.shape[-1]
    y = x * lax.rsqrt(jnp.mean(x * x, axis=-1, keepdims=True) + EPS) * g_ref[...]
    if modulated:
        sc = sc_ref[...].reshape(-1, d)
        sh = sh_ref[...].reshape(-1, d)
        y = y * (1.0 + sc) + sh
    o_ref[...] = y.astype(o_ref.dtype)
    if router:
        idx_ref, wt_ref = refs[i + 1], refs[i + 2]
        logits = jnp.dot(y, wr_ref[...], precision=HI, preferred_element_type=F32) + br_ref[...]
        lane = lax.broadcasted_iota(jnp.int32, logits.shape, 1)
        neg = jnp.float32(-jnp.inf)
        logits = jnp.where(lane < n_experts, logits, neg)
        m1 = jnp.max(logits, axis=-1, keepdims=True)
        i1 = jnp.min(jnp.where(logits == m1, lane, LANE), axis=-1, keepdims=True)
        rest = jnp.where(lane == i1, neg, logits)
        m2 = jnp.max(rest, axis=-1, keepdims=True)
        i2 = jnp.min(jnp.where(rest == m2, lane, LANE), axis=-1, keepdims=True)
        e2 = jnp.exp(m2 - m1)
        w1 = 1.0 / (1.0 + e2)
        w2 = e2 / (1.0 + e2)
        idx_ref[...] = jnp.where(lane == 0, i1, jnp.where(lane == 1, i2, 0))
        wt_ref[...] = jnp.where(lane == 0, w1, jnp.where(lane == 1, w2, 0.0))


def norm_mod(x, g, sc=None, sh=None, *, out_dtype, rows_per_mod=None, router=None, tm=512,
             name="norm"):
    M, D = x.shape
    tm = _pick_tile(M if rows_per_mod is None else rows_per_mod, tm)
    modulated = sc is not None
    in_specs = [pl.BlockSpec((tm, D), lambda i: (i, 0)), pl.BlockSpec((1, D), lambda i: (0, 0))]
    args = [x, g.reshape(1, D)]
    if modulated:
        if rows_per_mod is None:
            spec = pl.BlockSpec((tm, D), lambda i: (i, 0))
            in_specs += [spec, spec]
            args += [sc, sh]
        else:
            assert rows_per_mod % tm == 0
            rpt = rows_per_mod // tm
            spec = pl.BlockSpec((1, 1, D), lambda i: (i // rpt, 0, 0))
            in_specs += [spec, spec]
            args += [sc.reshape(-1, 1, D), sh.reshape(-1, 1, D)]
    out_shape = [jax.ShapeDtypeStruct((M, D), out_dtype)]
    out_specs = [pl.BlockSpec((tm, D), lambda i: (i, 0))]
    n_experts = 0
    if router is not None:
        w_r, b_r = router
        n_experts = w_r.shape[1]
        w_pad = jnp.zeros((D, LANE), F32).at[:, :n_experts].set(w_r)
        b_pad = jnp.zeros((1, LANE), F32).at[0, :n_experts].set(b_r)
        in_specs += [pl.BlockSpec((D, LANE), lambda i: (0, 0)), pl.BlockSpec((1, LANE), lambda i: (0, 0))]
        args += [w_pad, b_pad]
        out_shape += [jax.ShapeDtypeStruct((M, LANE), jnp.int32), jax.ShapeDtypeStruct((M, LANE), F32)]
        out_specs += [pl.BlockSpec((tm, LANE), lambda i: (i, 0))] * 2
    kern = functools.partial(_norm_kernel, modulated=modulated, router=router is not None,
                             n_experts=n_experts)
    res = pl.pallas_call(kern, grid=(M // tm,), in_specs=in_specs, out_specs=out_specs,
                         out_shape=out_shape, compiler_params=_cp("parallel"), name=name)(*args)
    return res if router is not None else res[0]


def _two_head_q(q):
    lane = lax.broadcasted_iota(jnp.int32, q.shape, 1)
    q0 = jnp.where(lane < 64, q, 0.0).astype(BF16)
    q1 = jnp.where(lane >= 64, q, 0.0).astype(BF16)
    return jnp.concatenate([q0, q1], axis=0)


def _sb_prompt_kernel(q_ref, k_ref, v_ref, g_ref, o_ref, acc_ref, carry_ref, *, t, n_pairs):
    qi = pl.program_id(1)
    r = lax.broadcasted_iota(jnp.int32, (t, t), 0)
    c = lax.broadcasted_iota(jnp.int32, (t, t), 1)
    upper = jnp.where(r > c, 1.0, 0.0).astype(BF16)
    upper2 = jnp.concatenate([upper, upper], axis=0)
    strict = jnp.concatenate([c < r, c < r], axis=0)
    n_tiles = t // LANE

    def lanes(x):
        return jnp.concatenate([x] * n_tiles, axis=1) if n_tiles > 1 else x
    qqs = [_two_head_q(q_ref[0, :, hp * LANE:(hp + 1) * LANE] * (1.0 / math.sqrt(64.0)))
           for hp in range(n_pairs)]

    def step(j, diagonal):
        start = pl.multiple_of(j * t, t)
        for hp in range(n_pairs):
            cols = slice(hp * LANE, (hp + 1) * LANE)
            k = k_ref[0, pl.ds(start, t), cols].astype(BF16)
            v = v_ref[0, pl.ds(start, t), cols].astype(BF16)
            z = lax.dot_general(qqs[hp], k, NT_DIMS, preferred_element_type=F32)
            lk = jnp.minimum(-z, 0.0) - jnp.log(1.0 + jnp.exp(-jnp.abs(z)))
            if diagonal:
                lk = jnp.where(strict, lk, 0.0)
            hi = lk.astype(BF16)
            lo = (lk - hi.astype(F32)).astype(BF16)
            la = jnp.dot(jnp.concatenate([hi, lo], axis=1), upper2, preferred_element_type=F32)
            total = jnp.broadcast_to(la[:, 0:1] + lk[:, 0:1], (2 * t, LANE))
            if diagonal:
                w = jnp.where(strict, jnp.exp(z + lk + la), 0.0)
                carry_ref[hp] = total
                acc_ref[hp] = jnp.dot(w.astype(BF16), v, preferred_element_type=F32)
            else:
                carry = carry_ref[hp]
                w = jnp.exp(z + lk + la + lanes(carry))
                carry_ref[hp] = carry + total
                acc_ref[hp] += jnp.dot(w.astype(BF16), v, preferred_element_type=F32)

    step(qi, True)

    def body(jj, _):
        step(qi - 1 - jj, False)
        return 0

    lax.fori_loop(0, qi, body, 0)
    lane = lax.broadcasted_iota(jnp.int32, (t, LANE), 1)
    mean64 = _group_mean_matrix(LANE, 64)
    for hp in range(n_pairs):
        acc = acc_ref[hp]
        o = jnp.where(lane < 64, acc[:t], acc[t:])
        ms = jnp.dot((o * o).astype(BF16), mean64.astype(BF16), preferred_element_type=F32)
        cols = slice(hp * LANE, (hp + 1) * LANE)
        o_ref[:, cols] = (o * lax.rsqrt(ms + EPS) * g_ref[:, cols]).astype(o_ref.dtype)


def sb_attention_prompt(z3, gain, B, T, *, t=256):
    nq = T // t
    width = z3.shape[2]
    n_pairs = width // LANE
    kern = functools.partial(_sb_prompt_kernel, t=t, n_pairs=n_pairs)
    return pl.pallas_call(
        kern, grid=(B, nq),
        in_specs=[pl.BlockSpec((1, t, width), lambda b, i: (0, b * nq + i, 0)),
                  pl.BlockSpec((1, T, width), lambda b, i: (1, b, 0)),
                  pl.BlockSpec((1, T, width), lambda b, i: (2, b, 0)),
                  pl.BlockSpec((1, width), lambda b, i: (0, 0))],
        out_specs=pl.BlockSpec((t, width), lambda b, i: (b * nq + i, 0)),
        out_shape=jax.ShapeDtypeStruct((B * T, width), BF16),
        scratch_shapes=[pltpu.VMEM((n_pairs, 2 * t, LANE), F32)] * 2,
        compiler_params=_cp("parallel", "arbitrary"), name="sb_prompt")(
            z3, z3, z3, gain.reshape(1, -1))


def _diff_lambda(lv_ref):
    lv = lv_ref[...]
    s1 = jnp.sum(lv[0:1] * lv[1:2], axis=-1, keepdims=True)
    s2 = jnp.sum(lv[2:3] * lv[3:4], axis=-1, keepdims=True)
    return jnp.exp(s1) - jnp.exp(s2)


def _diff_prompt_kernel(q_ref, k_ref, v_ref, g_ref, lv_ref, o_ref, acc_ref, m_ref, *, t,
                        n_heads, lam_init):
    qi = pl.program_id(1)
    assert t <= 256
    r = lax.broadcasted_iota(jnp.int32, (2 * t, t), 0)
    r = jnp.where(r >= t, r - t, r)
    c = lax.broadcasted_iota(jnp.int32, (2 * t, t), 1)
    causal = c <= r
    kc = lax.broadcasted_iota(jnp.int32, (t, LANE), 0)
    kl = lax.broadcasted_iota(jnp.int32, (t, LANE), 1)
    k_pos = jnp.where(kl == 0, kc // 16, jnp.where(kl == 1, kc % 16, 0)).astype(BF16)
    ql = lax.broadcasted_iota(jnp.int32, (2 * t, LANE), 1)
    neg = jnp.float32(-jnp.inf)
    slopes = [2.0 ** (-8.0 * (h + 1) / n_heads) for h in range(n_heads)]
    q_augs = []
    for h in range(n_heads):
        qq = _two_head_q(q_ref[0, :, h * LANE:(h + 1) * LANE] * (1.0 / math.sqrt(64.0)))
        q_pos = jnp.where(ql == 0, 16.0 * slopes[h], jnp.where(ql == 1, slopes[h], 0.0)).astype(BF16)
        q_augs.append(jnp.concatenate([qq, q_pos], axis=1))

    ones = jnp.ones((t, LANE), BF16)
    n_tiles = t // LANE

    def lanes(x):
        return jnp.concatenate([x] * n_tiles, axis=1) if n_tiles > 1 else x

    def row_max(s):
        part = s[:, 0:LANE]
        for i in range(1, n_tiles):
            part = jnp.maximum(part, s[:, i * LANE:(i + 1) * LANE])
        return jnp.broadcast_to(jnp.max(part, axis=-1, keepdims=True), part.shape)

    def step(j, diagonal):
        start = pl.multiple_of(j * t, t)
        for h in range(n_heads):
            cols = slice(h * LANE, (h + 1) * LANE)
            k = jnp.concatenate([k_ref[0, pl.ds(start, t), cols].astype(BF16), k_pos], axis=1)
            v = jnp.concatenate([v_ref[0, pl.ds(start, t), cols].astype(BF16), ones], axis=1)
            s = lax.dot_general(q_augs[h], k, NT_DIMS, preferred_element_type=F32)
            off = slopes[h] * (j * t).astype(F32)
            if diagonal:
                s = jnp.where(causal, s, neg)
                m_loc = row_max(s)
                p = jnp.exp(s - lanes(m_loc))
                m_ref[h] = m_loc + off
                acc_ref[h] = jnp.dot(p.astype(BF16), v, preferred_element_type=F32)
            else:
                m_old = m_ref[h]
                m_new = jnp.maximum(m_old, row_max(s) + off)
                alpha = jnp.exp(m_old - m_new)
                p = jnp.exp(s - lanes(m_new - off))
                m_ref[h] = m_new
                acc_ref[h] = (jnp.concatenate([alpha, alpha], axis=1) * acc_ref[h]
                              + jnp.dot(p.astype(BF16), v, preferred_element_type=F32))

    step(qi, True)

    def body(jj, _):
        step(qi - 1 - jj, False)
        return 0

    lax.fori_loop(0, qi, body, 0)
    lam = _diff_lambda(lv_ref) + lam_init
    for h in range(n_heads):
        cols = slice(h * LANE, (h + 1) * LANE)
        acc = acc_ref[h]
        on = acc[:, :LANE] / acc[:, LANE:]
        o = on[:t] - lam * on[t:]
        ms = jnp.mean(o * o, axis=-1, keepdims=True)
        o_ref[:, cols] = (o * lax.rsqrt(ms + EPS) * g_ref[:, cols] * (1.0 - lam_init)).astype(o_ref.dtype)


def diff_attention_prompt(z3, gain, lam_vec, lam_init, B, T, *, t=256):
    nq = T // t
    width = z3.shape[2]
    n_heads = width // LANE
    kern = functools.partial(_diff_prompt_kernel, t=t, n_heads=n_heads, lam_init=lam_init)
    return pl.pallas_call(
        kern, grid=(B, nq),
        in_specs=[pl.BlockSpec((1, t, width), lambda b, i: (3, b * nq + i, 0)),
                  pl.BlockSpec((1, T, width), lambda b, i: (4, b, 0)),
                  pl.BlockSpec((1, T, width), lambda b, i: (5, b, 0)),
                  pl.BlockSpec((1, width), lambda b, i: (0, 0)),
                  pl.BlockSpec(lam_vec.shape, lambda b, i: (0, 0))],
        out_specs=pl.BlockSpec((t, width), lambda b, i: (b * nq + i, 0)),
        out_shape=jax.ShapeDtypeStruct((B * T, width), BF16),
        scratch_shapes=[pltpu.VMEM((n_heads, 2 * t, 2 * LANE), F32),
                        pltpu.VMEM((n_heads, 2 * t, LANE), F32)],
        compiler_params=_cp("parallel", "arbitrary"), name="diff_prompt")(
            z3, z3, z3, gain.reshape(1, -1), lam_vec)


def _gelu_tanh(x):
    return 0.5 * x * (1.0 + jnp.tanh(math.sqrt(2.0 / math.pi) * (x + 0.044715 * (x * x * x))))


MXU_WIDTH = 256


def _blockdiag_dot(x, w, precise):
    width = x.shape[-1]
    if precise or width % MXU_WIDTH:
        return jnp.dot(x, w, precision=HI, preferred_element_type=F32)
    parts = [jnp.dot(x[:, s:s + MXU_WIDTH].astype(BF16), w[s:s + MXU_WIDTH, s:s + MXU_WIDTH].astype(BF16),
                     preferred_element_type=F32) for s in range(0, width, MXU_WIDTH)]
    return jnp.concatenate(parts, axis=1)


def _lru_gates(xc, wr_ref, br_ref, wi_ref, bi_ref, lam_ref, precise):
    r = _sigmoid(_blockdiag_dot(xc, wr_ref[...], precise) + br_ref[...])
    i = _sigmoid(_blockdiag_dot(xc, wi_ref[...], precise) + bi_ref[...])
    log_a = -LRU_C * r * _softplus(-lam_ref[...])
    a = jnp.exp(log_a)
    u = jnp.sqrt(1.0 - jnp.exp(2.0 * log_a)) * (i * xc)
    return a, u


def _lru_finish(h, gate, g_ref, width, precise):
    y = h * _gelu_tanh(gate)
    ms = _blockdiag_dot(y * y, _group_mean_matrix(width, 64), precise)
    return y * lax.rsqrt(ms + EPS) * g_ref[...]


def _lru_prompt_kernel(x_ref, gate_ref, cw_ref, cb_ref, wr_ref, br_ref, wi_ref, bi_ref, lam_ref,
                       g_ref, y_ref, conv_ref, h_ref, xs_ref, hprev_ref, *, tc, width):
    ti = pl.program_id(1)

    @pl.when(ti == 0)
    def _():
        xs_ref[0:8, :] = jnp.zeros((8, width), F32)
        hprev_ref[...] = jnp.zeros_like(hprev_ref)

    @pl.when(ti > 0)
    def _():
        xs_ref[0:8, :] = xs_ref[tc:tc + 8, :]

    x = x_ref[0]
    xs_ref[8:8 + tc, :] = x
    xc = cb_ref[...] + x * cw_ref[3:4, :]
    for j in range(3):
        xc = xc + xs_ref[5 + j:5 + j + tc, :] * cw_ref[j:j + 1, :]
    a, u = _lru_gates(xc, wr_ref, br_ref, wi_ref, bi_ref, lam_ref, precise=False)
    row = lax.broadcasted_iota(jnp.int32, (tc, width), 0)
    d = 1
    while d < tc:
        valid = row >= d
        a_s = pltpu.roll(a, d, 0)
        u_s = pltpu.roll(u, d, 0)
        u = jnp.where(valid, a * u_s + u, u)
        a = jnp.where(valid, a * a_s, a)
        d *= 2
    h = u + a * hprev_ref[0:1, :]
    hprev_ref[...] = jnp.broadcast_to(h[tc - 1:tc, :], hprev_ref.shape)
    y_ref[...] = _lru_finish(h, gate_ref[0], g_ref, width, precise=False).astype(y_ref.dtype)

    @pl.when(ti == pl.num_programs(1) - 1)
    def _():
        conv_ref[0] = xs_ref[tc:tc + 8, :]
        h_ref[0] = h[tc - 8:tc, :]


def _block_diag(w):
    n, k, _ = w.shape
    eye = jnp.eye(n, dtype=w.dtype)
    return jnp.einsum('nkj,nm->nkmj', w, eye).reshape(n * k, n * k)


def lru_prompt(z3, p, l, B, T, *, tc=256):
    width = z3.shape[2]
    nt = T // tc
    row = lambda a: a.reshape(1, width)
    const = lambda shape: pl.BlockSpec(shape, lambda b, i: (0,) * len(shape))
    kern = functools.partial(_lru_prompt_kernel, tc=tc, width=width)
    y, conv, h = pl.pallas_call(
        kern, grid=(B, nt),
        in_specs=[pl.BlockSpec((1, tc, width), lambda b, i: (6, b * nt + i, 0)),
                  pl.BlockSpec((1, tc, width), lambda b, i: (7, b * nt + i, 0)),
                  const((4, width)), const((1, width)), const((width, width)), const((1, width)),
                  const((width, width)), const((1, width)), const((1, width)), const((1, width))],
        out_specs=[pl.BlockSpec((tc, width), lambda b, i: (b * nt + i, 0)),
                   pl.BlockSpec((1, 8, width), lambda b, i: (b, 0, 0)),
                   pl.BlockSpec((1, 8, width), lambda b, i: (b, 0, 0))],
        out_shape=[jax.ShapeDtypeStruct((B * T, width), BF16),
                   jax.ShapeDtypeStruct((B, 8, width), F32),
                   jax.ShapeDtypeStruct((B, 8, width), F32)],
        scratch_shapes=[pltpu.VMEM((tc + 8, width), F32), pltpu.VMEM((8, width), F32)],
        compiler_params=_cp("parallel", "arbitrary"), name="lru_prompt")(
            z3, z3, p['lru_conv_w'][l], row(p['lru_conv_b'][l]), _block_diag(p['lru_w_r'][l]),
            row(p['lru_b_r'][l]), _block_diag(p['lru_w_i'][l]), row(p['lru_b_i'][l]),
            row(p['lru_lam'][l]), row(p['g_lru_out'][l]))
    return y, conv[:, 5:8, :], h[:, 7, :]


def _lru_step_kernel(x_ref, gate_ref, conv_ref, h0_ref, cw_ref, cb_ref, wr_ref, br_ref, wi_ref,
                     bi_ref, lam_ref, g_ref, y_ref, h_ref, *, width):
    xc = cb_ref[...] + x_ref[...] * cw_ref[3:4, :]
    for j in range(3):
        xc = xc + conv_ref[j] * cw_ref[j:j + 1, :]
    a, u = _lru_gates(xc, wr_ref, br_ref, wi_ref, bi_ref, lam_ref, precise=True)
    h = a * h0_ref[...] + u
    h_ref[...] = h
    y_ref[...] = _lru_finish(h, gate_ref[...], g_ref, width, precise=True)


def lru_step(x, gate, conv0, h0, p, l):
    G, width = x.shape
    row = lambda a: a.reshape(1, width)
    kern = functools.partial(_lru_step_kernel, width=width)
    y, h = pl.pallas_call(
        kern, out_shape=[jax.ShapeDtypeStruct((G, width), F32)] * 2, name="lru_step")(
            x, gate, conv0.swapaxes(0, 1), h0, p['lru_conv_w'][l], row(p['lru_conv_b'][l]),
            _block_diag(p['lru_w_r'][l]), row(p['lru_b_r'][l]), _block_diag(p['lru_w_i'][l]),
            row(p['lru_b_i'][l]), row(p['lru_lam'][l]), row(p['g_lru_out'][l]))
    conv_new = jnp.concatenate([conv0[:, 1:], x[:, None, :]], axis=1)
    return y, conv_new, h


def _log_sigmoid(x):
    return -_softplus(-x)


def _mlstm_prompt_kernel(q_ref, k_ref, v_ref, og_ref, gc_ref, gr_ref, g_ref, y_ref, c_out, n_out,
                         m_out, c_ref, n_ref, m_ref, *, L, nc, d, n_hh):
    c_ref[...] = jnp.zeros_like(c_ref)
    n_ref[...] = jnp.zeros_like(n_ref)
    m_ref[...] = jnp.zeros_like(m_ref)
    r = lax.broadcasted_iota(jnp.int32, (L, L), 0)
    cc = lax.broadcasted_iota(jnp.int32, (L, L), 1)
    causal = cc <= r
    neg = jnp.float32(-jnp.inf)

    def chunk_head(start, hh):
        cols = slice(hh * d, (hh + 1) * d)
        q = q_ref[0, pl.ds(start, L), cols]
        ks = k_ref[0, pl.ds(start, L), cols] * (1.0 / math.sqrt(d))
        v = v_ref[0, pl.ds(start, L), cols]
        gcol = gc_ref[0, hh, pl.ds(start, L), :]
        grow = gr_ref[0, hh, :, pl.ds(start, L)]
        i_col, i_row = gcol[:, 0:1], grow[0:1, :]
        lf_col, lf_row = _log_sigmoid(gcol[:, 1:2]), _log_sigmoid(grow[1:2, :])
        b_col = jnp.sum(jnp.where(causal, lf_row, 0.0), axis=-1, keepdims=True)
        b_row = jnp.sum(jnp.where(r <= cc, lf_col, 0.0), axis=0, keepdims=True)
        m_prev = m_ref[hh, 0:1, 0:1]
        dmat = jnp.where(causal, b_col - b_row + i_row, neg)
        m_t = jnp.maximum(b_col + m_prev, jnp.max(dmat, axis=-1, keepdims=True))
        w = jnp.exp(dmat - m_t)
        inter = jnp.exp(b_col + m_prev - m_t)
        qb = q.astype(BF16)
        s = lax.dot_general(qb, ks.astype(BF16), NT_DIMS, preferred_element_type=F32)
        sw = w * s
        c = c_ref[hh]
        n = n_ref[hh, 0:1, :]
        num = (jnp.dot(sw.astype(BF16), v.astype(BF16), preferred_element_type=F32)
               + inter * lax.dot_general(qb, c.astype(BF16), NT_DIMS, preferred_element_type=F32))
        den = jnp.sum(sw, axis=-1, keepdims=True) + inter * jnp.sum(q * n, axis=-1, keepdims=True)
        h = num / jnp.maximum(jnp.abs(den), jnp.exp(-m_t))
        m_new = m_t[L - 1:L, :]
        decay = inter[L - 1:L, :]
        w_last = jnp.exp(b_col[L - 1:L, :] - b_col + i_col - m_new)
        c_ref[hh] = decay * c + _dot3((v * w_last).T, ks)
        n_ref[hh] = jnp.broadcast_to(decay * n + jnp.sum(ks * w_last, axis=0, keepdims=True),
                                     n_ref.shape[1:])
        m_ref[hh] = jnp.broadcast_to(m_new, m_ref.shape[1:])
        y = h * _sigmoid(og_ref[0, pl.ds(start, L), cols])
        ms = jnp.mean(y * y, axis=-1, keepdims=True)
        y_ref[pl.ds(start, L), cols] = (y * lax.rsqrt(ms + EPS) * g_ref[:, cols]).astype(y_ref.dtype)

    def chunk(ci, _):
        start = pl.multiple_of(ci * L, L)
        for hh in range(n_hh):
            chunk_head(start, hh)
        return 0

    lax.fori_loop(0, nc, chunk, 0)
    c_out[0] = c_ref[...]
    n_out[0] = n_ref[...]
    m_out[0] = m_ref[...]


def mlstm_prompt(z3, gates, gain, B, T, *, L=128):
    H = z3.shape[2] // LANE
    d = LANE
    g4 = gates.reshape(B, T, 2, H)
    gcol = g4.transpose(0, 3, 1, 2)
    grow = g4.transpose(0, 3, 2, 1)
    n_hh = 2 if H % 2 == 0 else 1
    kern = functools.partial(_mlstm_prompt_kernel, L=L, nc=T // L, d=d, n_hh=n_hh)
    col = lambda grp: pl.BlockSpec((1, T, n_hh * d), lambda b, h: (grp, b, h))
    y, c, n, m = pl.pallas_call(
        kern, grid=(B, H // n_hh),
        in_specs=[col(8), col(9), col(10), col(11),
                  pl.BlockSpec((1, n_hh, T, 2), lambda b, h: (b, h, 0, 0)),
                  pl.BlockSpec((1, n_hh, 2, T), lambda b, h: (b, h, 0, 0)),
                  pl.BlockSpec((1, n_hh * d), lambda b, h: (0, h))],
        out_specs=[pl.BlockSpec((T, n_hh * d), lambda b, h: (b, h)),
                   pl.BlockSpec((1, n_hh, d, d), lambda b, h: (b, h, 0, 0)),
                   pl.BlockSpec((1, n_hh, 8, d), lambda b, h: (b, h, 0, 0)),
                   pl.BlockSpec((1, n_hh, 8, LANE), lambda b, h: (b, h, 0, 0))],
        out_shape=[jax.ShapeDtypeStruct((B * T, H * d), BF16),
                   jax.ShapeDtypeStruct((B, H, d, d), F32),
                   jax.ShapeDtypeStruct((B, H, 8, d), F32),
                   jax.ShapeDtypeStruct((B, H, 8, LANE), F32)],
        scratch_shapes=[pltpu.VMEM((n_hh, d, d), F32), pltpu.VMEM((n_hh, 8, d), F32),
                        pltpu.VMEM((n_hh, 8, LANE), F32)],
        compiler_params=_cp("parallel", "parallel"), name="mlstm_prompt")(
            z3, z3, z3, z3, gcol, grow, gain.reshape(1, -1))
    return y, c, n[:, :, 0, :], m[:, :, 0, 0]


def _mlstm_step_kernel(q_ref, k_ref, v_ref, og_ref, i_ref, f_ref, c_ref, n_ref, m_ref, g_ref,
                       y_ref, c_out, n_out, m_out, *, d):
    q, v = q_ref[0], v_ref[0]
    k = k_ref[0] * (1.0 / math.sqrt(d))
    i_pre, lf = i_ref[0], _log_sigmoid(f_ref[0])
    c, n, m = c_ref[0], n_ref[0], m_ref[0]
    m_t = jnp.maximum(lf + m, i_pre)
    w = jnp.exp(i_pre - m_t)
    inter = jnp.exp(lf + m - m_t)
    sw = w * jnp.sum(q * k, axis=-1, keepdims=True)
    eye = (lax.broadcasted_iota(jnp.int32, (1, d, d), 1) == lax.broadcasted_iota(jnp.int32, (1, d, d), 2))
    cq_col = jnp.sum(c * q, axis=-1, keepdims=True)
    cq = jnp.sum(jnp.where(eye, cq_col, 0.0), axis=1, keepdims=True)
    num = sw * v + inter * cq
    den = sw + inter * jnp.sum(n * q, axis=-1, keepdims=True)
    h = num / jnp.maximum(jnp.abs(den), jnp.exp(-m_t))
    v_col = jnp.sum(jnp.where(eye, v, 0.0), axis=-1, keepdims=True)
    c_out[0] = inter * c + (w * v_col) * k
    n_out[0] = inter * n + w * k
    m_out[0] = m_t
    y = h * _sigmoid(og_ref[0])
    ms = jnp.mean(y * y, axis=-1, keepdims=True)
    y_ref[0] = y * lax.rsqrt(ms + EPS) * g_ref[...]


def mlstm_step(q, k, v, og, i_pre, f_pre, c0, n0, m0, gain):
    G, H, d = n0.shape
    vec = lambda a: a.reshape(G, H, 1, d)
    sca = lambda a: a.reshape(G, H, 1, 1)
    vspec = pl.BlockSpec((1, H, 1, d), lambda b: (b, 0, 0, 0))
    sspec = pl.BlockSpec((1, H, 1, 1), lambda b: (b, 0, 0, 0))
    cspec = pl.BlockSpec((1, H, d, d), lambda b: (b, 0, 0, 0))
    kern = functools.partial(_mlstm_step_kernel, d=d)
    y, c, n, m = pl.pallas_call(
        kern, grid=(G,),
        in_specs=[vspec, vspec, vspec, vspec, sspec, sspec, cspec, vspec, sspec,
                  pl.BlockSpec((H, 1, d), lambda b: (0, 0, 0))],
        out_specs=[vspec, cspec, vspec, sspec],
        out_shape=[jax.ShapeDtypeStruct((G, H, 1, d), F32), jax.ShapeDtypeStruct((G, H, d, d), F32),
                   jax.ShapeDtypeStruct((G, H, 1, d), F32), jax.ShapeDtypeStruct((G, H, 1, 1), F32)],
        compiler_params=_cp("parallel"), name="mlstm_step")(
            vec(q), vec(k), vec(v), vec(og), sca(i_pre), sca(f_pre), c0, vec(n0), sca(m0),
            gain.reshape(H, 1, d))
    return y.reshape(G, H * d), c, n.reshape(G, H, d), m.reshape(G, H)


def _page_specs(P, n_pages, block):
    zeros = (0,) * (len(block) - 1)

    def spec(p):
        return pl.BlockSpec(block, lambda b, s, rows: (rows[b * n_pages + s * P + p],) + zeros)
    return [spec(p) for p in range(P)]


def _decode_scores_kernel(rows_ref, q_ref, *rest, P, page):
    pages, o_ref = rest[:P], rest[P]
    n_grp, d = q_ref.shape[1], q_ref.shape[2]
    q = jnp.broadcast_to(q_ref[0] * (1.0 / math.sqrt(d)), (n_grp, d, page))
    for p in range(P):
        o_ref[0, :, :, p * page:(p + 1) * page] = jnp.sum(pages[p][0] * q, axis=1, keepdims=True)


def decode_scores(q, pool_t, rows, n_pages, *, P=32):
    G = q.shape[0]
    _, n_grp, d, page = pool_t.shape
    P = _pick_tile(n_pages, P)
    kern = functools.partial(_decode_scores_kernel, P=P, page=page)
    grid_spec = pltpu.PrefetchScalarGridSpec(
        num_scalar_prefetch=1, grid=(G, n_pages // P),
        in_specs=[pl.BlockSpec((1, n_grp, d, 1), lambda b, s, rows: (b, 0, 0, 0))]
        + _page_specs(P, n_pages, (1, n_grp, d, page)),
        out_specs=pl.BlockSpec((1, n_grp, 1, P * page), lambda b, s, rows: (b, 0, 0, s)))
    return pl.pallas_call(
        kern, grid_spec=grid_spec,
        out_shape=jax.ShapeDtypeStruct((G, n_grp, 1, n_pages * page), F32),
        compiler_params=_cp("parallel", "arbitrary"), name="decode_scores")(
            rows, q.reshape(G, n_grp, d, 1), *([pool_t] * P))


def _sb_weights_kernel(z_ref, w_ref, *, n_tiles):
    bb = z_ref.shape[0]
    r = lax.broadcasted_iota(jnp.int32, (LANE, LANE), 0)
    c = lax.broadcasted_iota(jnp.int32, (LANE, LANE), 1)
    upper = jnp.where(r > c, 1.0, 0.0).astype(F32)
    carry = jnp.zeros((bb * 8, 1), F32)
    for t in reversed(range(n_tiles)):
        z = z_ref[:, :, t * LANE:(t + 1) * LANE].reshape(bb * 8, LANE)
        lk = -_softplus(z)
        la = jnp.dot(lk, upper, precision=HI, preferred_element_type=F32) + carry
        w_ref[:, :, t * LANE:(t + 1) * LANE] = jnp.exp(z + lk + la).reshape(bb, 8, LANE)
        carry = carry + jnp.sum(lk, axis=-1, keepdims=True)


def sb_weights(z, *, bb=8):
    G, _, Tp = z.shape
    bb = _pick_tile(G, bb)
    kern = functools.partial(_sb_weights_kernel, n_tiles=Tp // LANE)
    spec = pl.BlockSpec((bb, 8, Tp), lambda i: (i, 0, 0))
    return pl.pallas_call(kern, grid=(G // bb,), in_specs=[spec], out_specs=spec,
                          out_shape=jax.ShapeDtypeStruct(z.shape, F32),
                          compiler_params=_cp("parallel"), name="sb_weights")(z)


def _diff_weights_kernel(z_ref, q_ref, k_ref, v_ref, lv_ref, w_ref, new_ref, *, lam_init):
    bb, _, Tp = z_ref.shape
    width = q_ref.shape[-1]
    lam = (_diff_lambda(lv_ref) + lam_init).reshape(1, 1, 1)
    prod = q_ref[...] * k_ref[...] * (1.0 / math.sqrt(64.0))
    grp = lax.broadcasted_iota(jnp.int32, (1, 8, width), 2) // 64
    r8 = lax.broadcasted_iota(jnp.int32, (1, 8, width), 1)
    mine = grp == (r8 % 4) * 2 + r8 // 4
    s_new = jnp.sum(jnp.where(mine, prod, 0.0), axis=-1, keepdims=True)
    head = lax.broadcasted_iota(jnp.int32, (1, 8, 1), 1) % 4
    slope = jnp.exp((-2.0 * math.log(2.0)) * (head + 1).astype(F32))
    kpos = lax.broadcasted_iota(jnp.int32, (1, 1, Tp), 2)
    s = z_ref[...] - slope * (Tp - kpos).astype(F32)
    m = jnp.maximum(jnp.max(s, axis=-1, keepdims=True), s_new)
    p = jnp.exp(s - m)
    p_new = jnp.exp(s_new - m)
    den = jnp.sum(p, axis=-1, keepdims=True) + p_new
    p = p / den
    p_new = p_new / den
    w = p[:, 0:4, :] - lam * p[:, 4:8, :]
    w8 = jnp.concatenate([w, jnp.zeros_like(w)], axis=1).reshape(bb * 8, Tp)
    n_heads = 4
    spread = (lax.broadcasted_iota(jnp.int32, (LANE, n_heads * LANE), 1) // n_heads
              == lax.broadcasted_iota(jnp.int32, (LANE, n_heads * LANE), 0)).astype(F32)
    keep = (lax.broadcasted_iota(jnp.int32, (bb * 8, n_heads * LANE), 1) % n_heads
            == lax.broadcasted_iota(jnp.int32, (bb * 8, n_heads * LANE), 0) % 8)
    for t in range(Tp // LANE):
        wide = jnp.dot(w8[:, t * LANE:(t + 1) * LANE], spread, precision=HI,
                       preferred_element_type=F32)
        w_ref[:, :, t * n_heads * LANE:(t + 1) * n_heads * LANE] = jnp.where(keep, wide, 0.0).reshape(
            bb, 8, n_heads * LANE)
    w_new = p_new[:, 0:4, :] - lam * p_new[:, 4:8, :]
    new_ref[...] = w_new * v_ref[...]


def diff_weights(z, q, k_new, v_new, lam_vec, lam_init, *, bb=8):
    G, _, Tp = z.shape
    width = q.shape[-1]
    bb = _pick_tile(G, bb)
    kern = functools.partial(_diff_weights_kernel, lam_init=lam_init)
    zspec = pl.BlockSpec((bb, 8, Tp), lambda i: (i, 0, 0))
    qspec = pl.BlockSpec((bb, 1, width), lambda i: (i, 0, 0))
    vspec = pl.BlockSpec((bb, 4, LANE), lambda i: (i, 0, 0))
    r3 = lambda a: a.reshape(G, 1, width)
    return pl.pallas_call(
        kern, grid=(G // bb,),
        in_specs=[zspec, qspec, qspec, vspec, pl.BlockSpec(lam_vec.shape, lambda i: (0, 0))],
        out_specs=[pl.BlockSpec((bb, 8, 4 * Tp), lambda i: (i, 0, 0)), vspec],
        out_shape=[jax.ShapeDtypeStruct((G, 8, 4 * Tp), F32), jax.ShapeDtypeStruct((G, 4, LANE), F32)],
        compiler_params=_cp("parallel"), name="diff_weights")(
            z, r3(q), r3(k_new), v_new.reshape(G, 4, LANE), lam_vec)


def _decode_pv_t_kernel(rows_ref, w_ref, *rest, P, page):
    pages, o_ref, acc_ref = rest[:P], rest[P], rest[P + 1]
    s = pl.program_id(1)

    @pl.when(s == 0)
    def _():
        acc_ref[...] = jnp.zeros_like(acc_ref)

    acc = acc_ref[...]
    for p in range(P):
        acc = acc + pages[p][0] * w_ref[0, :, :, p * page:(p + 1) * page]
    acc_ref[...] = acc

    @pl.when(s == pl.num_programs(1) - 1)
    def _():
        d = acc.shape[1]
        col = jnp.sum(acc, axis=-1, keepdims=True)
        eye = (lax.broadcasted_iota(jnp.int32, (1, d, d), 1)
               == lax.broadcasted_iota(jnp.int32, (1, d, d), 2))
        o_ref[0] = jnp.sum(jnp.where(eye, col, 0.0), axis=1, keepdims=True)


def decode_pv_t(w, pool_t, rows, n_pages, *, P=32):
    G = w.shape[0]
    _, n_grp, d, page = pool_t.shape
    P = _pick_tile(n_pages, P)
    kern = functools.partial(_decode_pv_t_kernel, P=P, page=page)
    grid_spec = pltpu.PrefetchScalarGridSpec(
        num_scalar_prefetch=1, grid=(G, n_pages // P),
        in_specs=[pl.BlockSpec((1, n_grp, 1, P * page), lambda b, s, rows: (b, 0, 0, s))]
        + _page_specs(P, n_pages, (1, n_grp, d, page)),
        out_specs=pl.BlockSpec((1, n_grp, 1, d), lambda b, s, rows: (b, 0, 0, 0)),
        scratch_shapes=[pltpu.VMEM((n_grp, d, page), F32)])
    out = pl.pallas_call(
        kern, grid_spec=grid_spec, out_shape=jax.ShapeDtypeStruct((G, n_grp, 1, d), F32),
        compiler_params=_cp("parallel", "arbitrary"), name="decode_pv_t")(rows, w, *([pool_t] * P))
    return out.reshape(G, n_grp * d)


def _decode_pv_rows_kernel(rows_ref, w_ref, init_ref, *rest, P, n_rows):
    pages, o_ref, acc_ref = rest[:P], rest[P], rest[P + 1]
    s = pl.program_id(1)

    @pl.when(s == 0)
    def _():
        acc_ref[...] = jnp.zeros_like(acc_ref)

    acc = acc_ref[...]
    for p in range(P):
        acc = acc + _dot3(w_ref[0, :, p * n_rows:(p + 1) * n_rows], pages[p][0])
    acc_ref[...] = acc

    @pl.when(s == pl.num_programs(1) - 1)
    def _():
        o_ref[0] = acc[0:4] + init_ref[0]


def decode_pv_rows(w, pool, rows, init, n_pages, *, P=32):
    G = w.shape[0]
    _, n_rows, width = pool.shape
    P = _pick_tile(n_pages, P)
    kern = functools.partial(_decode_pv_rows_kernel, P=P, n_rows=n_rows)
    grid_spec = pltpu.PrefetchScalarGridSpec(
        num_scalar_prefetch=1, grid=(G, n_pages // P),
        in_specs=[pl.BlockSpec((1, 8, P * n_rows), lambda b, s, rows: (b, 0, s)),
                  pl.BlockSpec((1, 4, width), lambda b, s, rows: (b, 0, 0))]
        + _page_specs(P, n_pages, (1, n_rows, width)),
        out_specs=pl.BlockSpec((1, 4, width), lambda b, s, rows: (b, 0, 0)),
        scratch_shapes=[pltpu.VMEM((8, width), F32)])
    out = pl.pallas_call(
        kern, grid_spec=grid_spec, out_shape=jax.ShapeDtypeStruct((G, 4, width), F32),
        compiler_params=_cp("parallel", "arbitrary"), name="decode_pv_rows")(
            rows, w, init, *([pool] * P))
    return out.reshape(G, 4 * width)


def _group_norm_kernel(x_ref, g_ref, o_ref, *, group, post_scale):
    x = x_ref[...]
    ms = jnp.dot(x * x, _group_mean_matrix(x.shape[-1], group), precision=HI,
                 preferred_element_type=F32)
    o_ref[...] = x * lax.rsqrt(ms + EPS) * g_ref[...] * post_scale


def _moe_up_kernel(te_ref, act_ref, first_ref, a_ref, wg_ref, wu_ref, o_ref, wg_bf, wu_bf):
    i = pl.program_id(1)

    @pl.when(first_ref[i] > 0)
    def _():
        wg_bf[...] = wg_ref[...].astype(BF16)
        wu_bf[...] = wu_ref[...].astype(BF16)

    @pl.when(act_ref[i] > 0)
    def _():
        a = a_ref[...].astype(BF16)
        g = jnp.dot(a, wg_bf[...], preferred_element_type=F32)
        u = jnp.dot(a, wu_bf[...], preferred_element_type=F32)
        o_ref[...] = (g * _sigmoid(g) * u).astype(o_ref.dtype)

    @pl.when(act_ref[i] == 0)
    def _():
        o_ref[...] = jnp.zeros_like(o_ref)


def _moe_down_kernel(te_ref, act_ref, first_ref, a_ref, wd_ref, o_ref, wd_bf):
    i = pl.program_id(1)

    @pl.when(first_ref[i] > 0)
    def _():
        wd_bf[...] = wd_ref[...].astype(BF16)

    @pl.when(act_ref[i] > 0)
    def _():
        o_ref[...] = jnp.dot(a_ref[...], wd_bf[...], preferred_element_type=F32)

    @pl.when(act_ref[i] == 0)
    def _():
        o_ref[...] = jnp.zeros_like(o_ref)


def moe_grouped(xg, tile_expert, tile_active, tile_first, wg, wu, wd, *, tg, tn_up=1408, tn_down=1024):
    n_rows, D = xg.shape
    F = wg.shape[-1]
    n_tiles = n_rows // tg
    tn_up, tn_down = _pick_tile(F, tn_up), _pick_tile(D, tn_down)
    once = pl.Buffered(1)
    up_spec = pltpu.PrefetchScalarGridSpec(
        num_scalar_prefetch=3, grid=(F // tn_up, n_tiles),
        in_specs=[pl.BlockSpec((tg, D), lambda j, i, te, ac, fi: (i, 0)),
                  pl.BlockSpec((None, D, tn_up), lambda j, i, te, ac, fi: (te[i], 0, j), pipeline_mode=once),
                  pl.BlockSpec((None, D, tn_up), lambda j, i, te, ac, fi: (te[i], 0, j), pipeline_mode=once)],
        out_specs=pl.BlockSpec((tg, tn_up), lambda j, i, te, ac, fi: (i, j)),
        scratch_shapes=[pltpu.VMEM((D, tn_up), BF16), pltpu.VMEM((D, tn_up), BF16)])
    act = pl.pallas_call(_moe_up_kernel, grid_spec=up_spec,
                         out_shape=jax.ShapeDtypeStruct((n_rows, F), BF16),
                         compiler_params=_cp("arbitrary", "arbitrary"), name="moe_up")(
                             tile_expert, tile_active, tile_first, xg, wg, wu)
    down_spec = pltpu.PrefetchScalarGridSpec(
        num_scalar_prefetch=3, grid=(D // tn_down, n_tiles),
        in_specs=[pl.BlockSpec((tg, F), lambda j, i, te, ac, fi: (i, 0)),
                  pl.BlockSpec((None, F, tn_down), lambda j, i, te, ac, fi: (te[i], 0, j), pipeline_mode=once)],
        out_specs=pl.BlockSpec((tg, tn_down), lambda j, i, te, ac, fi: (i, j)),
        scratch_shapes=[pltpu.VMEM((F, tn_down), BF16)])
    return pl.pallas_call(_moe_down_kernel, grid_spec=down_spec,
                          out_shape=jax.ShapeDtypeStruct((n_rows, D), F32),
                          compiler_params=_cp("arbitrary", "arbitrary"), name="moe_down")(
                              tile_expert, tile_active, tile_first, act, wd)


def _moe_dispatch(idx, n_experts, tg):
    M = idx.shape[0]
    flat_e = idx.reshape(-1)
    onehot = (flat_e[:, None] == jnp.arange(n_experts, dtype=jnp.int32)[None, :]).astype(jnp.int32)
    counts = jnp.sum(onehot, axis=0)
    rank = jnp.sum((jnp.cumsum(onehot, axis=0) - onehot) * onehot, axis=1)
    padded = (counts + tg - 1) // tg * tg
    ends = jnp.cumsum(padded)
    pos = (ends - padded)[flat_e] + rank
    n_rows = 2 * M + n_experts * tg
    row_token = jnp.zeros((n_rows,), jnp.int32).at[pos].set(jnp.arange(2 * M, dtype=jnp.int32) // 2)
    tile_start = jnp.arange(n_rows // tg, dtype=jnp.int32) * tg
    tile_expert = jnp.minimum(jnp.sum((tile_start[:, None] >= ends[None, :]).astype(jnp.int32), axis=1),
                              n_experts - 1)
    tile_expert = tile_expert.astype(jnp.int32)
    tile_active = (tile_start < ends[-1]).astype(jnp.int32)
    tile_first = jnp.concatenate([jnp.ones((1,), jnp.int32),
                                  (tile_expert[1:] != tile_expert[:-1]).astype(jnp.int32)])
    return pos.reshape(M, 2), row_token, tile_expert, tile_active, tile_first


def _moe_combine_kernel(x_ref, g_ref, y1_ref, y2_ref, w_ref, o_ref):
    w = w_ref[...]
    f = w[:, 0:1] * y1_ref[...] + w[:, 1:2] * y2_ref[...]
    o_ref[...] = x_ref[...] + g_ref[0] * f


def moe_combine(x, g, y1, y2, wts, rows_per_gate, *, tm=512):
    M, D = x.shape
    tm = _pick_tile(rows_per_gate, tm)
    rpt = rows_per_gate // tm
    row = pl.BlockSpec((tm, D), lambda i: (i, 0))
    return pl.pallas_call(
        _moe_combine_kernel, grid=(M // tm,),
        in_specs=[row, pl.BlockSpec((1, 1, D), lambda i: (i // rpt, 0, 0)), row, row,
                  pl.BlockSpec((tm, LANE), lambda i: (i, 0))],
        out_specs=row, out_shape=jax.ShapeDtypeStruct((M, D), F32),
        compiler_params=_cp("parallel"), name="moe_combine")(x, g.reshape(-1, 1, D), y1, y2, wts)


def _moe_dense_up_kernel(a_ref, wg_ref, wu_ref, o_ref, accg_ref, accu_ref, *, nk):
    k = pl.program_id(2)
    a = a_ref[...]
    pg = _dot3(a, wg_ref[...])
    pu = _dot3(a, wu_ref[...])

    @pl.when(k == 0)
    def _():
        accg_ref[...] = pg
        accu_ref[...] = pu

    @pl.when(k > 0)
    def _():
        accg_ref[...] += pg
        accu_ref[...] += pu

    @pl.when(k == nk - 1)
    def _():
        g = accg_ref[...]
        o_ref[...] = g * _sigmoid(g) * accu_ref[...]


def _moe_dense_down_kernel(a_ref, wd_ref, cw_ref, x_ref, g_ref, o_ref, acc_ref):
    e, k = pl.program_id(0), pl.program_id(1)
    part = cw_ref[...] * _dot3(a_ref[...], wd_ref[...])

    @pl.when((e == 0) & (k == 0))
    def _():
        acc_ref[...] = part

    @pl.when((e > 0) | (k > 0))
    def _():
        acc_ref[...] += part

    @pl.when((e == pl.num_programs(0) - 1) & (k == pl.num_programs(1) - 1))
    def _():
        o_ref[...] = x_ref[...] + g_ref[...] * acc_ref[...]


def moe_dense(h, x, gate, combine, wg, wu, wd, *, tn=1408, tk=1024, tkd=1408):
    M, D = h.shape
    E, _, F = wg.shape
    tn, tk, tkd = _pick_tile(F, tn), _pick_tile(D, tk), _pick_tile(F, tkd)
    nk = D // tk
    wspec = pl.BlockSpec((None, tk, tn), lambda e, j, k: (e, k, j))
    act = pl.pallas_call(
        functools.partial(_moe_dense_up_kernel, nk=nk), grid=(E, F // tn, nk),
        in_specs=[pl.BlockSpec((M, tk), lambda e, j, k: (0, k)), wspec, wspec],
        out_specs=pl.BlockSpec((None, M, tn), lambda e, j, k: (e, 0, j)),
        out_shape=jax.ShapeDtypeStruct((E, M, F), F32),
        scratch_shapes=[pltpu.VMEM((M, tn), F32), pltpu.VMEM((M, tn), F32)],
        compiler_params=_cp("parallel", "parallel", "arbitrary"), name="moe_dense_up")(h, wg, wu)
    full = pl.BlockSpec((M, D), lambda e, k: (0, 0))
    return pl.pallas_call(
        _moe_dense_down_kernel, grid=(E, F // tkd),
        in_specs=[pl.BlockSpec((None, M, tkd), lambda e, k: (e, 0, k)),
                  pl.BlockSpec((None, tkd, D), lambda e, k: (e, k, 0)),
                  pl.BlockSpec((None, M, 1), lambda e, k: (e, 0, 0)), full, full],
        out_specs=full, out_shape=jax.ShapeDtypeStruct((M, D), F32),
        scratch_shapes=[pltpu.VMEM((M, D), F32)],
        compiler_params=_cp("arbitrary", "arbitrary"), name="moe_dense_down")(
            act, wd, combine, x, gate)


def group_norm(x, gain, *, group, post_scale=1.0):
    kern = functools.partial(_group_norm_kernel, group=group, post_scale=post_scale)
    return pl.pallas_call(kern, out_shape=jax.ShapeDtypeStruct(x.shape, F32), name="group_norm")(
        x, gain.reshape(1, -1))


W_GROUP = 512
N_MAIN = 12 * W_GROUP
MOE_TILE = 256


def _lam_init(l):
    return 0.8 - 0.6 * math.exp(-0.3 * l)


def _gate_weight(w_in, l):
    wg = w_in.swapaxes(1, 2)[l, N_MAIN:, :].T
    return jnp.zeros((wg.shape[0], LANE), F32).at[:, :wg.shape[1]].set(wg)


def _prompt_trunk(x3, mods, p, wb):
    B, T, D = x3.shape
    M = B * T
    x = x3.reshape(M, D)
    news = []
    for l in range(len(mods)):
        sh1, sc1, g1, sh2, sc2, g2 = mods[l]
        h = norm_mod(x, p['g_norm1'][l], sc1, sh1, out_dtype=BF16, rows_per_mod=T, name="norm1_p")
        z3, sbk_t, sbv_t, dfk_t = w_in_prompt(h, wb['w_in'], l, B, T, n_groups=12, width=W_GROUP,
                                              kt_groups=(1, 2, 4))
        gates = matmul(h, _gate_weight(p['w_in'], l).astype(BF16), precise=False, name="gates_p")
        gates = gates[:, :8] + jnp.concatenate([p['ml_b_i'][l], p['ml_b_f'][l]])[None, :]
        y_sb = sb_attention_prompt(z3, p['g_sb_out'][l], B, T)
        y_df = diff_attention_prompt(z3, p['g_diff_out'][l], p['diff_lam'][l], _lam_init(l), B, T)
        y_lru, conv_new, h_new = lru_prompt(z3, p, l, B, T)
        y_ml, c_new, n_new, m_new = mlstm_prompt(z3, gates, p['g_ml_out'][l], B, T)
        ycat = jnp.concatenate([y_sb, y_df, y_lru, y_ml], axis=-1)
        x = matmul(ycat, wb['w_out'], precise=False, layer=l, resid=(x, g1), rows_per_gate=T,
                   name="w_out_p")
        j = l // 2
        if l % 2 == 0:
            h2 = norm_mod(x, p['g_norm2'][l], sc2, sh2, out_dtype=BF16, rows_per_mod=T, name="norm2_p")
            act = swiglu_up(h2, wb['ffn_w_gate'], wb['ffn_w_up'], precise=False, out_dtype=BF16,
                            layer=j, name="ffn_up_p")
            x = matmul(act, wb['ffn_w_down'], precise=False, layer=j, tk=1408, resid=(x, g2),
                       rows_per_gate=T, name="ffn_down_p")
        else:
            h2, idx, wts = norm_mod(x, p['g_norm2'][l], sc2, sh2, out_dtype=F32, rows_per_mod=T,
                                    router=(p['moe_w_router'][j], p['moe_b_router'][j]),
                                    name="norm2_router_p")
            n_experts = p['moe_w_router'].shape[-1]
            pos, row_token, tile_expert, tile_active, tile_first = _moe_dispatch(
                idx[:, :2], n_experts, MOE_TILE)
            take = lambda a, rows: a.at[rows].get(mode="promise_in_bounds")
            xg = take(h2, row_token)
            yg = moe_grouped(xg, tile_expert, tile_active, tile_first, p['moe_w_gate'][j],
                             p['moe_w_up'][j], p['moe_w_down'][j], tg=MOE_TILE)
            x = moe_combine(x, g2, take(yg, pos[:, 0]), take(yg, pos[:, 1]), wts, T)
        news.append((sbk_t.reshape(B, 8, 64, T).transpose(0, 3, 1, 2),
                     sbv_t.reshape(B, 8, 64, T).transpose(0, 3, 1, 2),
                     dfk_t.reshape(B, 4, 2, 64, T).transpose(0, 4, 1, 2, 3),
                     z3[5].reshape(B, T, 4, 128), conv_new, h_new, c_new, n_new, m_new))
    y = norm_mod(x, p['g_final'], out_dtype=F32, name="final_norm_p").reshape(B, T, D)
    return y, tuple(jnp.stack([nw[i] for nw in news], axis=0) for i in range(9))


def _sample_trunk(x3, mods, p, caches, states, page_table):
    G, _, D = x3.shape
    x = x3.reshape(G, D)
    sb_k_c, sb_v_c, df_k_c, df_v_c = caches
    n_layers, n_phys, page = sb_k_c.shape[:3]
    n_pool = n_layers * n_phys
    pool_sb_k = sb_k_c.transpose(0, 1, 3, 4, 2).reshape(n_pool, 8, 64, page)
    pool_sb_v = sb_v_c.transpose(0, 1, 3, 4, 2).reshape(n_pool, 8, 64, page)
    pool_df_k = df_k_c.transpose(0, 1, 3, 4, 5, 2).reshape(n_pool, 8, 64, page)
    pool_df_v = df_v_c.reshape(n_pool, page * 4, LANE)
    n_pages = page_table.shape[1]
    conv_all, h_all, c_all, n_all, m_all = states
    news = []
    for l in range(len(mods)):
        sh1, sc1, g1, sh2, sc2, g2 = mods[l]
        rows = (page_table + l * n_phys).reshape(-1).astype(jnp.int32)
        h = norm_mod(x, p['g_norm1'][l], sc1, sh1, out_dtype=F32, name="norm1_s")
        z = matmul(h, p['w_in'].swapaxes(1, 2), precise=True, layer=l, n_cols=N_MAIN, tn=W_GROUP,
                   b_transposed=True, name="w_in_s")
        gates = matmul(h, _gate_weight(p['w_in'], l), precise=True, name="gates_s")
        zs = [z[:, i * W_GROUP:(i + 1) * W_GROUP] for i in range(12)]
        sb_q, sb_k, sb_v, df_q, df_k, df_v, lru_x, lru_g, ml_q, ml_k, ml_v, ml_o = zs
        ml_i = gates[:, 0:4] + p['ml_b_i'][l][None, :]
        ml_f = gates[:, 4:8] + p['ml_b_f'][l][None, :]
        Tp = n_pages * page
        z_sb = decode_scores(sb_q, pool_sb_k, rows, n_pages).reshape(G, 8, Tp)
        w_sb = sb_weights(z_sb).reshape(G, 8, 1, Tp)
        y_sb = decode_pv_t(w_sb, pool_sb_v, rows, n_pages)
        y_sb = group_norm(y_sb, p['g_sb_out'][l], group=64)
        z_df = decode_scores(df_q, pool_df_k, rows, n_pages)
        z_df = z_df.reshape(G, 4, 2, Tp).swapaxes(1, 2).reshape(G, 8, Tp)
        w_df, new_df = diff_weights(z_df, df_q, df_k, df_v, p['diff_lam'][l], _lam_init(l))
        y_df = decode_pv_rows(w_df, pool_df_v, rows, new_df, n_pages)
        y_df = group_norm(y_df, p['g_diff_out'][l], group=128, post_scale=1.0 - _lam_init(l))
        y_lru, conv_new, h_new = lru_step(lru_x, lru_g, conv_all[l], h_all[l], p, l)
        y_ml, c_new, n_new, m_new = mlstm_step(ml_q, ml_k, ml_v, ml_o, ml_i, ml_f, c_all[l],
                                               n_all[l], m_all[l], p['g_ml_out'][l])
        ycat = jnp.concatenate([y_sb, y_df, y_lru, y_ml], axis=-1)
        x = matmul(ycat, p['w_out'], precise=True, layer=l, resid=(x, g1), name="w_out_s")
        j = l // 2
        if l % 2 == 0:
            h2 = norm_mod(x, p['g_norm2'][l], sc2, sh2, out_dtype=F32, name="norm2_s")
            act = swiglu_up(h2, p['ffn_w_gate'], p['ffn_w_up'], precise=True, out_dtype=F32,
                            layer=j, tk=1024, name="ffn_up_s")
            x = matmul(act, p['ffn_w_down'], precise=True, layer=j, tk=1408, resid=(x, g2),
                       name="ffn_down_s")
        else:
            h2, idx, wts = norm_mod(x, p['g_norm2'][l], sc2, sh2, out_dtype=F32,
                                    router=(p['moe_w_router'][j], p['moe_b_router'][j]),
                                    name="norm2_router_s")
            n_experts = p['moe_w_router'].shape[-1]
            e_ids = jnp.arange(n_experts, dtype=jnp.int32)[:, None]
            combine = (jnp.where(idx[None, :, 0] == e_ids, wts[None, :, 0], 0.0)
                       + jnp.where(idx[None, :, 1] == e_ids, wts[None, :, 1], 0.0))[..., None]
            x = moe_dense(h2, x, g2, combine, p['moe_w_gate'][j], p['moe_w_up'][j], p['moe_w_down'][j])
        news.append((sb_k.reshape(G, 1, 8, 64), sb_v.reshape(G, 1, 8, 64),
                     df_k.reshape(G, 1, 4, 2, 64), df_v.reshape(G, 1, 4, 128),
                     conv_new, h_new, c_new, n_new, m_new))
    y = norm_mod(x, p['g_final'], out_dtype=F32, name="final_norm_s").reshape(G, 1, D)
    return y, tuple(jnp.stack([nw[i] for nw in news], axis=0) for i in range(9))


def kernel(x_prompt, x_sample, cache_sb_k, cache_sb_v, cache_diff_k, cache_diff_v,
           state_lru_conv, state_lru_h, state_mlstm_c, state_mlstm_n, state_mlstm_m,
           page_table, c_prompt, c_sample, w_ada, b_ada, g_norm1, g_norm2, w_in, w_out,
           g_sb_out, diff_lam, g_diff_out, lru_conv_w, lru_conv_b, lru_w_r, lru_b_r,
           lru_w_i, lru_b_i, lru_lam, g_lru_out, ml_b_i, ml_b_f, g_ml_out,
           ffn_w_gate, ffn_w_up, ffn_w_down, moe_w_router, moe_b_router,
           moe_w_gate, moe_w_up, moe_w_down, g_final):
    p = dict(w_ada=w_ada, b_ada=b_ada, g_norm1=g_norm1, g_norm2=g_norm2, w_in=w_in, w_out=w_out,
             g_sb_out=g_sb_out, diff_lam=diff_lam, g_diff_out=g_diff_out, lru_conv_w=lru_conv_w,
             lru_conv_b=lru_conv_b, lru_w_r=lru_w_r, lru_b_r=lru_b_r, lru_w_i=lru_w_i,
             lru_b_i=lru_b_i, lru_lam=lru_lam, g_lru_out=g_lru_out, ml_b_i=ml_b_i, ml_b_f=ml_b_f,
             g_ml_out=g_ml_out, ffn_w_gate=ffn_w_gate, ffn_w_up=ffn_w_up, ffn_w_down=ffn_w_down,
             moe_w_router=moe_w_router, moe_b_router=moe_b_router, moe_w_gate=moe_w_gate,
             moe_w_up=moe_w_up, moe_w_down=moe_w_down, g_final=g_final)
    depth, D = g_norm1.shape
    Bp, Gs = x_prompt.shape[0], x_sample.shape[0]
    n_c = Bp + Gs
    c_all = jnp.zeros(((n_c + 15) // 16 * 16, D), F32).at[:Bp].set(c_prompt).at[Bp:n_c].set(c_sample)
    mods_p, mods_s = [], []
    for l in range(depth):
        m = matmul(c_all, w_ada, precise=True, layer=l, bias=b_ada[l], silu_a=True, name="ada")
        mods_p.append([m[:Bp, i * D:(i + 1) * D] for i in range(6)])
        mods_s.append([m[Bp:n_c, i * D:(i + 1) * D] for i in range(6)])
    wb = {k: p[k].astype(BF16) for k in ('w_in', 'w_out', 'ffn_w_gate', 'ffn_w_up', 'ffn_w_down')}
    y_p, new_p = _prompt_trunk(x_prompt, mods_p, p, wb)
    y_s, new_s = _sample_trunk(x_sample, mods_s, p,
                               (cache_sb_k, cache_sb_v, cache_diff_k, cache_diff_v),
                               (state_lru_conv, state_lru_h, state_mlstm_c, state_mlstm_n,
                                state_mlstm_m), page_table)
    return (y_p, y_s) + new_p + new_s
```

```python
import functools
import math

import jax
import jax.numpy as jnp
from jax import lax
from jax.experimental import pallas as pl
from jax.experimental.pallas import tpu as pltpu

F32 = jnp.float32
BF16 = jnp.bfloat16
HI = lax.Precision.HIGHEST
EPS = 1e-6
LRU_C = 8.0
LANE = 128
VMEM_LIMIT = 56 * 1024 * 1024

NT_DIMS = (((1,), (1,)), ((), ()))
F32_EXP_ZERO = -104.0


def _cp(*sem):
    return pltpu.CompilerParams(dimension_semantics=sem, vmem_limit_bytes=VMEM_LIMIT)


def _split(x):
    hi = x.astype(BF16)
    return hi, (x - hi.astype(F32)).astype(BF16)


def _dot3(a, b, dims=(((1,), (0,)), ((), ()))):
    m = a.shape[0]
    a_hi = a.astype(BF16)
    a_parts = jnp.concatenate([a, a - a_hi.astype(F32)], axis=0).astype(BF16)
    b_hi, b_lo = _split(b)
    r = lax.dot_general(a_parts, b_hi, dims, preferred_element_type=F32)
    return r[:m] + r[m:] + lax.dot_general(a_hi, b_lo, dims, preferred_element_type=F32)


def _dot(a, b, precise):
    if precise:
        return _dot3(a, b)
    return jnp.dot(a.astype(BF16), b.astype(BF16), preferred_element_type=F32)


def _softplus(x):
    return jnp.maximum(x, 0.0) + jnp.log1p(jnp.exp(-jnp.abs(x)))


def _sigmoid(x):
    return 1.0 / (1.0 + jnp.exp(-x))


def _group_mean_matrix(width, group):
    r = lax.broadcasted_iota(jnp.int32, (width, width), 0) // group
    c = lax.broadcasted_iota(jnp.int32, (width, width), 1) // group
    return jnp.where(r == c, 1.0 / group, 0.0).astype(F32)


def _pick_tile(n, pref):
    t = min(n, pref)
    while n % t:
        t //= 2
    return t


def _mm_kernel(*refs, nk, precise, has_bias, silu_a, resid, b_transposed):
    a_ref, b_ref = refs[0], refs[1]
    i = 2
    bias_ref = x_ref = g_ref = None
    if has_bias:
        bias_ref = refs[i]
        i += 1
    if resid:
        x_ref, g_ref = refs[i], refs[i + 1]
        i += 2
    o_ref, acc_ref = refs[i], refs[i + 1]
    k = pl.program_id(2)
    a = a_ref[...]
    if silu_a:
        a = a * _sigmoid(a)
    if b_transposed:
        assert precise
        part = _dot3(a, b_ref[...], NT_DIMS)
    else:
        part = _dot(a, b_ref[...], precise)

    def finish(acc):
        if has_bias:
            acc = acc + bias_ref[...]
        if resid:
            g = g_ref[...].reshape(-1, acc.shape[-1])
            acc = x_ref[...] + g * acc
        o_ref[...] = acc.reshape(o_ref.shape).astype(o_ref.dtype)

    if nk == 1:
        finish(part)
    else:
        @pl.when(k == 0)
        def _():
            acc_ref[...] = part

        @pl.when(k > 0)
        def _():
            acc_ref[...] += part

        @pl.when(k == nk - 1)
        def _():
            finish(acc_ref[...])


def _weight_spec(b, tk, tn, layer, transposed=False):
    if transposed:
        return pl.BlockSpec((None, tn, tk), lambda i, j, k: (layer, j, k))
    if b.ndim == 2:
        return pl.BlockSpec((tk, tn), lambda i, j, k: (k, j))
    return pl.BlockSpec((None, tk, tn), lambda i, j, k: (layer, k, j))


def matmul(a, b, *, precise, out_dtype=F32, tm=1024, tn=1024, tk=2048, bias=None, silu_a=False,
           out3d_width=None, resid=None, rows_per_gate=None, layer=0, n_cols=None,
           b_transposed=False, name="mm"):
    M, K = a.shape
    N = n_cols if n_cols is not None else (b.shape[-2] if b_transposed else b.shape[-1])
    tm, tn, tk = _pick_tile(M, tm), _pick_tile(N, tn), _pick_tile(K, tk)
    if rows_per_gate is not None:
        tm = _pick_tile(rows_per_gate, tm)
    nk = K // tk
    in_specs = [pl.BlockSpec((tm, tk), lambda i, j, k: (i, k)),
                _weight_spec(b, tk, tn, layer, b_transposed)]
    args = [a, b]
    if bias is not None:
        in_specs.append(pl.BlockSpec((1, tn), lambda i, j, k: (0, j)))
        args.append(bias.reshape(1, N))
    if resid is not None:
        x, g = resid
        in_specs.append(pl.BlockSpec((tm, tn), lambda i, j, k: (i, j)))
        args.append(x)
        if rows_per_gate is None:
            in_specs.append(pl.BlockSpec((tm, tn), lambda i, j, k: (i, j)))
            args.append(g)
        else:
            assert rows_per_gate % tm == 0
            rpt = rows_per_gate // tm
            in_specs.append(pl.BlockSpec((1, 1, tn), lambda i, j, k: (i // rpt, 0, j)))
            args.append(g.reshape(g.shape[0], 1, N))
    if out3d_width is None:
        out_shape = jax.ShapeDtypeStruct((M, N), out_dtype)
        out_spec = pl.BlockSpec((tm, tn), lambda i, j, k: (i, j))
    else:
        assert tn == out3d_width
        out_shape = jax.ShapeDtypeStruct((N // tn, M, tn), out_dtype)
        out_spec = pl.BlockSpec((1, tm, tn), lambda i, j, k: (j, i, 0))
    kern = functools.partial(_mm_kernel, nk=nk, precise=precise, has_bias=bias is not None,
                             silu_a=silu_a, resid=resid is not None, b_transposed=b_transposed)
    return pl.pallas_call(
        kern, grid=(M // tm, N // tn, nk), in_specs=in_specs, out_specs=out_spec,
        out_shape=out_shape, scratch_shapes=[pltpu.VMEM((tm, tn), F32)],
        compiler_params=_cp("parallel", "parallel", "arbitrary"), name=name)(*args)


def _w_in_prompt_kernel(a_ref, b_ref, z_ref, *kt_refs, kt_groups):
    res = jnp.dot(a_ref[...], b_ref[...], preferred_element_type=F32)
    z_ref[0] = res
    j = pl.program_id(1)
    for grp, ref in zip(kt_groups, kt_refs):
        @pl.when(j == grp)
        def _(ref=ref):
            ref[0] = res.T


def w_in_prompt(h, w, layer, B, T, *, n_groups, width, kt_groups, tm=1024):
    M, K = h.shape
    tm = _pick_tile(T, tm)
    per_seq = T // tm
    kern = functools.partial(_w_in_prompt_kernel, kt_groups=kt_groups)
    kt_spec = pl.BlockSpec((1, width, tm), lambda i, j: (i // per_seq, 0, i % per_seq))
    return pl.pallas_call(
        kern, grid=(M // tm, n_groups),
        in_specs=[pl.BlockSpec((tm, K), lambda i, j: (i, 0)),
                  pl.BlockSpec((None, K, width), lambda i, j: (layer, 0, j))],
        out_specs=[pl.BlockSpec((1, tm, width), lambda i, j: (j, i, 0))] + [kt_spec] * len(kt_groups),
        out_shape=[jax.ShapeDtypeStruct((n_groups, M, width), F32)]
        + [jax.ShapeDtypeStruct((B, width, T), F32)] * len(kt_groups),
        compiler_params=_cp("parallel", "arbitrary"), name="w_in_p")(h, w)


def _swiglu_kernel(a_ref, wg_ref, wu_ref, o_ref, accg_ref, accu_ref, *, nk, precise):
    k = pl.program_id(2)
    a = a_ref[...]
    pg = _dot(a, wg_ref[...], precise)
    pu = _dot(a, wu_ref[...], precise)

    def finish(g, u):
        o_ref[...] = (g * _sigmoid(g) * u).astype(o_ref.dtype)

    if nk == 1:
        finish(pg, pu)
    else:
        @pl.when(k == 0)
        def _():
            accg_ref[...] = pg
            accu_ref[...] = pu

        @pl.when(k > 0)
        def _():
            accg_ref[...] += pg
            accu_ref[...] += pu

        @pl.when(k == nk - 1)
        def _():
            finish(accg_ref[...], accu_ref[...])


def swiglu_up(a, wg, wu, *, precise, out_dtype, tm=1024, tn=512, tk=2048, layer=0, name="swiglu_up"):
    M, K = a.shape
    N = wg.shape[-1]
    tm, tn, tk = _pick_tile(M, tm), _pick_tile(N, tn), _pick_tile(K, tk)
    nk = K // tk
    kern = functools.partial(_swiglu_kernel, nk=nk, precise=precise)
    return pl.pallas_call(
        kern, grid=(M // tm, N // tn, nk),
        in_specs=[pl.BlockSpec((tm, tk), lambda i, j, k: (i, k)),
                  _weight_spec(wg, tk, tn, layer), _weight_spec(wu, tk, tn, layer)],
        out_specs=pl.BlockSpec((tm, tn), lambda i, j, k: (i, j)),
        out_shape=jax.ShapeDtypeStruct((M, N), out_dtype),
        scratch_shapes=[pltpu.VMEM((tm, tn), F32), pltpu.VMEM((tm, tn), F32)],
        compiler_params=_cp("parallel", "parallel", "arbitrary"), name=name)(a, wg, wu)


def _norm_kernel(*refs, modulated, router, n_experts):
    x_ref, g_ref = refs[0], refs[1]
    i = 2
    if modulated:
        sc_ref, sh_ref = refs[i], refs[i + 1]
        i += 2
    if router:
        wr_ref, br_ref = refs[i], refs[i + 1]
        i += 2
    o_ref = refs[i]
    x = x_ref[...]
    d = x.shape[-1]
    y = x * lax.rsqrt(jnp.mean(x * x, axis=-1, keepdims=True) + EPS) * g_ref[...]
    if modulated:
        sc = sc_ref[...].reshape(-1, d)
        sh = sh_ref[...].reshape(-1, d)
        y = y * (1.0 + sc) + sh
    o_ref[...] = y.astype(o_ref.dtype)
    if router:
        idx_ref, wt_ref = refs[i + 1], refs[i + 2]
        logits = jnp.dot(y, wr_ref[...], precision=HI, preferred_element_type=F32) + br_ref[...]
        lane = lax.broadcasted_iota(jnp.int32, logits.shape, 1)
        neg = jnp.float32(-jnp.inf)
        logits = jnp.where(lane < n_experts, logits, neg)
        m1 = jnp.max(logits, axis=-1, keepdims=True)
        i1 = jnp.min(jnp.where(logits == m1, lane, LANE), axis=-1, keepdims=True)
        rest = jnp.where(lane == i1, neg, logits)
        m2 = jnp.max(rest, axis=-1, keepdims=True)
        i2 = jnp.min(jnp.where(rest == m2, lane, LANE), axis=-1, keepdims=True)
        e2 = jnp.exp(m2 - m1)
        w1 = 1.0 / (1.0 + e2)
        w2 = e2 / (1.0 + e2)
        idx_ref[...] = jnp.where(lane == 0, i1, jnp.where(lane == 1, i2, 0))
        wt_ref[...] = jnp.where(lane == 0, w1, jnp.where(lane == 1, w2, 0.0))


def norm_mod(x, g, sc=None, sh=None, *, out_dtype, rows_per_mod=None, router=None, tm=512,
             name="norm"):
    M, D = x.shape
    tm = _pick_tile(M if rows_per_mod is None else rows_per_mod, tm)
    modulated = sc is not None
    in_specs = [pl.BlockSpec((tm, D), lambda i: (i, 0)), pl.BlockSpec((1, D), lambda i: (0, 0))]
    args = [x, g.reshape(1, D)]
    if modulated:
        if rows_per_mod is None:
            spec = pl.BlockSpec((tm, D), lambda i: (i, 0))
            in_specs += [spec, spec]
            args += [sc, sh]
        else:
            assert rows_per_mod % tm == 0
            rpt = rows_per_mod // tm
            spec = pl.BlockSpec((1, 1, D), lambda i: (i // rpt, 0, 0))
            in_specs += [spec, spec]
            args += [sc.reshape(-1, 1, D), sh.reshape(-1, 1, D)]
    out_shape = [jax.ShapeDtypeStruct((M, D), out_dtype)]
    out_specs = [pl.BlockSpec((tm, D), lambda i: (i, 0))]
    n_experts = 0
    if router is not None:
        w_r, b_r = router
        n_experts = w_r.shape[1]
        w_pad = jnp.zeros((D, LANE), F32).at[:, :n_experts].set(w_r)
        b_pad = jnp.zeros((1, LANE), F32).at[0, :n_experts].set(b_r)
        in_specs += [pl.BlockSpec((D, LANE), lambda i: (0, 0)), pl.BlockSpec((1, LANE), lambda i: (0, 0))]
        args += [w_pad, b_pad]
        out_shape += [jax.ShapeDtypeStruct((M, LANE), jnp.int32), jax.ShapeDtypeStruct((M, LANE), F32)]
        out_specs += [pl.BlockSpec((tm, LANE), lambda i: (i, 0))] * 2
    kern = functools.partial(_norm_kernel, modulated=modulated, router=router is not None,
                             n_experts=n_experts)
    res = pl.pallas_call(kern, grid=(M // tm,), in_specs=in_specs, out_specs=out_specs,
                         out_shape=out_shape, compiler_params=_cp("parallel"), name=name)(*args)
    return res if router is not None else res[0]


def _two_head_q(q):
    lane = lax.broadcasted_iota(jnp.int32, q.shape, 1)
    q0 = jnp.where(lane < 64, q, 0.0).astype(BF16)
    q1 = jnp.where(lane >= 64, q, 0.0).astype(BF16)
    return jnp.concatenate([q0, q1], axis=0)


def _sb_prompt_kernel(q_ref, k_ref, v_ref, g_ref, o_ref, acc_ref, carry_ref, *, t, n_pairs):
    qi = pl.program_id(1)
    r = lax.broadcasted_iota(jnp.int32, (t, t), 0)
    c = lax.broadcasted_iota(jnp.int32, (t, t), 1)
    upper = jnp.where(r > c, 1.0, 0.0).astype(BF16)
    upper2 = jnp.concatenate([upper, upper], axis=0)
    strict = jnp.concatenate([c < r, c < r], axis=0)
    n_tiles = t // LANE

    def lanes(x):
        return jnp.concatenate([x] * n_tiles, axis=1) if n_tiles > 1 else x
    qqs = [_two_head_q(q_ref[0, :, hp * LANE:(hp + 1) * LANE] * (1.0 / math.sqrt(64.0)))
           for hp in range(n_pairs)]

    def step(j, diagonal):
        start = pl.multiple_of(j * t, t)
        for hp in range(n_pairs):
            cols = slice(hp * LANE, (hp + 1) * LANE)
            k = k_ref[0, pl.ds(start, t), cols].astype(BF16)
            v = v_ref[0, pl.ds(start, t), cols].astype(BF16)
            z = lax.dot_general(qqs[hp], k, NT_DIMS, preferred_element_type=F32)
            lk = jnp.minimum(-z, 0.0) - jnp.log(1.0 + jnp.exp(-jnp.abs(z)))
            if diagonal:
                lk = jnp.where(strict, lk, 0.0)
            hi = lk.astype(BF16)
            lo = (lk - hi.astype(F32)).astype(BF16)
            la = jnp.dot(jnp.concatenate([hi, lo], axis=1), upper2, preferred_element_type=F32)
            total = jnp.broadcast_to(la[:, 0:1] + lk[:, 0:1], (2 * t, LANE))
            if diagonal:
                w = jnp.where(strict, jnp.exp(z + lk + la), 0.0)
                carry_ref[hp] = total
                acc_ref[hp] = jnp.dot(w.astype(BF16), v, preferred_element_type=F32)
            else:
                carry = carry_ref[hp]
                w = jnp.exp(z + lk + la + lanes(carry))
                carry_ref[hp] = carry + total
                acc_ref[hp] += jnp.dot(w.astype(BF16), v, preferred_element_type=F32)

    step(qi, True)

    def any_row_alive():
        top = carry_ref[0]
        for hp in range(1, n_pairs):
            top = jnp.maximum(top, carry_ref[hp])
        return (jnp.max(top) > F32_EXP_ZERO).astype(jnp.int32)

    def body(jj, alive):
        @pl.when(alive > 0)
        def _():
            step(qi - 1 - jj, False)
        return any_row_alive()

    lax.fori_loop(0, qi, body, any_row_alive())
    lane = lax.broadcasted_iota(jnp.int32, (t, LANE), 1)
    mean64 = _group_mean_matrix(LANE, 64)
    for hp in range(n_pairs):
        acc = acc_ref[hp]
        o = jnp.where(lane < 64, acc[:t], acc[t:])
        ms = jnp.dot((o * o).astype(BF16), mean64.astype(BF16), preferred_element_type=F32)
        cols = slice(hp * LANE, (hp + 1) * LANE)
        o_ref[:, cols] = (o * lax.rsqrt(ms + EPS) * g_ref[:, cols]).astype(o_ref.dtype)


def sb_attention_prompt(z3, gain, B, T, *, t=256):
    nq = T // t
    width = z3.shape[2]
    n_pairs = width // LANE
    kern = functools.partial(_sb_prompt_kernel, t=t, n_pairs=n_pairs)
    return pl.pallas_call(
        kern, grid=(B, nq),
        in_specs=[pl.BlockSpec((1, t, width), lambda b, i: (0, b * nq + i, 0)),
                  pl.BlockSpec((1, T, width), lambda b, i: (1, b, 0)),
                  pl.BlockSpec((1, T, width), lambda b, i: (2, b, 0)),
                  pl.BlockSpec((1, width), lambda b, i: (0, 0))],
        out_specs=pl.BlockSpec((t, width), lambda b, i: (b * nq + i, 0)),
        out_shape=jax.ShapeDtypeStruct((B * T, width), BF16),
        scratch_shapes=[pltpu.VMEM((n_pairs, 2 * t, LANE), F32)] * 2,
        compiler_params=_cp("parallel", "arbitrary"), name="sb_prompt")(
            z3, z3, z3, gain.reshape(1, -1))


def _diff_lambda(lv_ref):
    lv = lv_ref[...]
    s1 = jnp.sum(lv[0:1] * lv[1:2], axis=-1, keepdims=True)
    s2 = jnp.sum(lv[2:3] * lv[3:4], axis=-1, keepdims=True)
    return jnp.exp(s1) - jnp.exp(s2)


def _diff_prompt_kernel(q_ref, k_ref, v_ref, g_ref, lv_ref, o_ref, acc_ref, m_ref, *, t,
                        n_heads, lam_init):
    qi = pl.program_id(1)
    assert t <= 256
    r = lax.broadcasted_iota(jnp.int32, (2 * t, t), 0)
    r = jnp.where(r >= t, r - t, r)
    c = lax.broadcasted_iota(jnp.int32, (2 * t, t), 1)
    causal = c <= r
    kc = lax.broadcasted_iota(jnp.int32, (t, LANE), 0)
    kl = lax.broadcasted_iota(jnp.int32, (t, LANE), 1)
    k_pos = jnp.where(kl == 0, kc // 16, jnp.where(kl == 1, kc % 16, 0)).astype(BF16)
    ql = lax.broadcasted_iota(jnp.int32, (2 * t, LANE), 1)
    neg = jnp.float32(-jnp.inf)
    slopes = [2.0 ** (-8.0 * (h + 1) / n_heads) for h in range(n_heads)]
    q_augs = []
    for h in range(n_heads):
        qq = _two_head_q(q_ref[0, :, h * LANE:(h + 1) * LANE] * (1.0 / math.sqrt(64.0)))
        q_pos = jnp.where(ql == 0, 16.0 * slopes[h], jnp.where(ql == 1, slopes[h], 0.0)).astype(BF16)
        q_augs.append(jnp.concatenate([qq, q_pos], axis=1))

    ones = jnp.ones((t, LANE), BF16)
    n_tiles = t // LANE

    def lanes(x):
        return jnp.concatenate([x] * n_tiles, axis=1) if n_tiles > 1 else x

    def row_max(s):
        part = s[:, 0:LANE]
        for i in range(1, n_tiles):
            part = jnp.maximum(part, s[:, i * LANE:(i + 1) * LANE])
        return jnp.broadcast_to(jnp.max(part, axis=-1, keepdims=True), part.shape)

    def step(j, diagonal):
        start = pl.multiple_of(j * t, t)
        for h in range(n_heads):
            cols = slice(h * LANE, (h + 1) * LANE)
            k = jnp.concatenate([k_ref[0, pl.ds(start, t), cols].astype(BF16), k_pos], axis=1)
            v = jnp.concatenate([v_ref[0, pl.ds(start, t), cols].astype(BF16), ones], axis=1)
            s = lax.dot_general(q_augs[h], k, NT_DIMS, preferred_element_type=F32)
            off = slopes[h] * (j * t).astype(F32)
            if diagonal:
                s = jnp.where(causal, s, neg)
                m_loc = row_max(s)
                p = jnp.exp(s - lanes(m_loc))
                m_ref[h] = m_loc + off
                acc_ref[h] = jnp.dot(p.astype(BF16), v, preferred_element_type=F32)
            else:
                m_old = m_ref[h]
                m_new = jnp.maximum(m_old, row_max(s) + off)
                alpha = jnp.exp(m_old - m_new)
                p = jnp.exp(s - lanes(m_new - off))
                m_ref[h] = m_new
                acc_ref[h] = (jnp.concatenate([alpha, alpha], axis=1) * acc_ref[h]
                              + jnp.dot(p.astype(BF16), v, preferred_element_type=F32))

    step(qi, True)

    def body(jj, _):
        step(qi - 1 - jj, False)
        return 0

    lax.fori_loop(0, qi, body, 0)
    lam = _diff_lambda(lv_ref) + lam_init
    for h in range(n_heads):
        cols = slice(h * LANE, (h + 1) * LANE)
        acc = acc_ref[h]
        on = acc[:, :LANE] / acc[:, LANE:]
        o = on[:t] - lam * on[t:]
        ms = jnp.mean(o * o, axis=-1, keepdims=True)
        o_ref[:, cols] = (o * lax.rsqrt(ms + EPS) * g_ref[:, cols] * (1.0 - lam_init)).astype(o_ref.dtype)


def diff_attention_prompt(z3, gain, lam_vec, lam_init, B, T, *, t=256):
    nq = T // t
    width = z3.shape[2]
    n_heads = width // LANE
    kern = functools.partial(_diff_prompt_kernel, t=t, n_heads=n_heads, lam_init=lam_init)
    return pl.pallas_call(
        kern, grid=(B, nq),
        in_specs=[pl.BlockSpec((1, t, width), lambda b, i: (3, b * nq + i, 0)),
                  pl.BlockSpec((1, T, width), lambda b, i: (4, b, 0)),
                  pl.BlockSpec((1, T, width), lambda b, i: (5, b, 0)),
                  pl.BlockSpec((1, width), lambda b, i: (0, 0)),
                  pl.BlockSpec(lam_vec.shape, lambda b, i: (0, 0))],
        out_specs=pl.BlockSpec((t, width), lambda b, i: (b * nq + i, 0)),
        out_shape=jax.ShapeDtypeStruct((B * T, width), BF16),
        scratch_shapes=[pltpu.VMEM((n_heads, 2 * t, 2 * LANE), F32),
                        pltpu.VMEM((n_heads, 2 * t, LANE), F32)],
        compiler_params=_cp("parallel", "arbitrary"), name="diff_prompt")(
            z3, z3, z3, gain.reshape(1, -1), lam_vec)


def _gelu_tanh(x):
    return 0.5 * x * (1.0 + jnp.tanh(math.sqrt(2.0 / math.pi) * (x + 0.044715 * (x * x * x))))


MXU_WIDTH = 256


def _blockdiag_dot(x, w, precise):
    width = x.shape[-1]
    if precise or width % MXU_WIDTH:
        return jnp.dot(x, w, precision=HI, preferred_element_type=F32)
    parts = [jnp.dot(x[:, s:s + MXU_WIDTH].astype(BF16), w[s:s + MXU_WIDTH, s:s + MXU_WIDTH].astype(BF16),
                     preferred_element_type=F32) for s in range(0, width, MXU_WIDTH)]
    return jnp.concatenate(parts, axis=1)


def _lru_gates(xc, wr_ref, br_ref, wi_ref, bi_ref, lam_ref, precise):
    r = _sigmoid(_blockdiag_dot(xc, wr_ref[...], precise) + br_ref[...])
    i = _sigmoid(_blockdiag_dot(xc, wi_ref[...], precise) + bi_ref[...])
    log_a = -LRU_C * r * _softplus(-lam_ref[...])
    a = jnp.exp(log_a)
    u = jnp.sqrt(1.0 - jnp.exp(2.0 * log_a)) * (i * xc)
    return a, u


def _lru_finish(h, gate, g_ref, width, precise):
    y = h * _gelu_tanh(gate)
    ms = _blockdiag_dot(y * y, _group_mean_matrix(width, 64), precise)
    return y * lax.rsqrt(ms + EPS) * g_ref[...]


def _lru_prompt_kernel(x_ref, gate_ref, cw_ref, cb_ref, wr_ref, br_ref, wi_ref, bi_ref, lam_ref,
                       g_ref, y_ref, conv_ref, h_ref, xs_ref, hprev_ref, *, tc, width):
    ti = pl.program_id(1)

    @pl.when(ti == 0)
    def _():
        xs_ref[0:8, :] = jnp.zeros((8, width), F32)
        hprev_ref[...] = jnp.zeros_like(hprev_ref)

    @pl.when(ti > 0)
    def _():
        xs_ref[0:8, :] = xs_ref[tc:tc + 8, :]

    x = x_ref[0]
    xs_ref[8:8 + tc, :] = x
    xc = cb_ref[...] + x * cw_ref[3:4, :]
    for j in range(3):
        xc = xc + xs_ref[5 + j:5 + j + tc, :] * cw_ref[j:j + 1, :]
    a, u = _lru_gates(xc, wr_ref, br_ref, wi_ref, bi_ref, lam_ref, precise=False)
    row = lax.broadcasted_iota(jnp.int32, (tc, width), 0)
    d = 1
    while d < tc:
        valid = row >= d
        a_s = pltpu.roll(a, d, 0)
        u_s = pltpu.roll(u, d, 0)
        u = jnp.where(valid, a * u_s + u, u)
        a = jnp.where(valid, a * a_s, a)
        d *= 2
    h = u + a * hprev_ref[0:1, :]
    hprev_ref[...] = jnp.broadcast_to(h[tc - 1:tc, :], hprev_ref.shape)
    y_ref[...] = _lru_finish(h, gate_ref[0], g_ref, width, precise=False).astype(y_ref.dtype)

    @pl.when(ti == pl.num_programs(1) - 1)
    def _():
        conv_ref[0] = xs_ref[tc:tc + 8, :]
        h_ref[0] = h[tc - 8:tc, :]


def _block_diag(w):
    n, k, _ = w.shape
    eye = jnp.eye(n, dtype=w.dtype)
    return jnp.einsum('nkj,nm->nkmj', w, eye).reshape(n * k, n * k)


def lru_prompt(z3, p, l, B, T, *, tc=256):
    width = z3.shape[2]
    nt = T // tc
    row = lambda a: a.reshape(1, width)
    const = lambda shape: pl.BlockSpec(shape, lambda b, i: (0,) * len(shape))
    kern = functools.partial(_lru_prompt_kernel, tc=tc, width=width)
    y, conv, h = pl.pallas_call(
        kern, grid=(B, nt),
        in_specs=[pl.BlockSpec((1, tc, width), lambda b, i: (6, b * nt + i, 0)),
                  pl.BlockSpec((1, tc, width), lambda b, i: (7, b * nt + i, 0)),
                  const((4, width)), const((1, width)), const((width, width)), const((1, width)),
                  const((width, width)), const((1, width)), const((1, width)), const((1, width))],
        out_specs=[pl.BlockSpec((tc, width), lambda b, i: (b * nt + i, 0)),
                   pl.BlockSpec((1, 8, width), lambda b, i: (b, 0, 0)),
                   pl.BlockSpec((1, 8, width), lambda b, i: (b, 0, 0))],
        out_shape=[jax.ShapeDtypeStruct((B * T, width), BF16),
                   jax.ShapeDtypeStruct((B, 8, width), F32),
                   jax.ShapeDtypeStruct((B, 8, width), F32)],
        scratch_shapes=[pltpu.VMEM((tc + 8, width), F32), pltpu.VMEM((8, width), F32)],
        compiler_params=_cp("parallel", "arbitrary"), name="lru_prompt")(
            z3, z3, p['lru_conv_w'][l], row(p['lru_conv_b'][l]), _block_diag(p['lru_w_r'][l]),
            row(p['lru_b_r'][l]), _block_diag(p['lru_w_i'][l]), row(p['lru_b_i'][l]),
            row(p['lru_lam'][l]), row(p['g_lru_out'][l]))
    return y, conv[:, 5:8, :], h[:, 7, :]


def _lru_step_kernel(x_ref, gate_ref, conv_ref, h0_ref, cw_ref, cb_ref, wr_ref, br_ref, wi_ref,
                     bi_ref, lam_ref, g_ref, y_ref, h_ref, *, width):
    xc = cb_ref[...] + x_ref[...] * cw_ref[3:4, :]
    for j in range(3):
        xc = xc + conv_ref[j] * cw_ref[j:j + 1, :]
    a, u = _lru_gates(xc, wr_ref, br_ref, wi_ref, bi_ref, lam_ref, precise=True)
    h = a * h0_ref[...] + u
    h_ref[...] = h
    y_ref[...] = _lru_finish(h, gate_ref[...], g_ref, width, precise=True)


def lru_step(x, gate, conv0, h0, p, l):
    G, width = x.shape
    row = lambda a: a.reshape(1, width)
    kern = functools.partial(_lru_step_kernel, width=width)
    y, h = pl.pallas_call(
        kern, out_shape=[jax.ShapeDtypeStruct((G, width), F32)] * 2, name="lru_step")(
            x, gate, conv0.swapaxes(0, 1), h0, p['lru_conv_w'][l], row(p['lru_conv_b'][l]),
            _block_diag(p['lru_w_r'][l]), row(p['lru_b_r'][l]), _block_diag(p['lru_w_i'][l]),
            row(p['lru_b_i'][l]), row(p['lru_lam'][l]), row(p['g_lru_out'][l]))
    conv_new = jnp.concatenate([conv0[:, 1:], x[:, None, :]], axis=1)
    return y, conv_new, h


def _log_sigmoid(x):
    return -_softplus(-x)


def _mlstm_prompt_kernel(q_ref, k_ref, v_ref, og_ref, gc_ref, gr_ref, g_ref, y_ref, c_out, n_out,
                         m_out, c_ref, n_ref, m_ref, *, L, nc, d, n_hh):
    c_ref[...] = jnp.zeros_like(c_ref)
    n_ref[...] = jnp.zeros_like(n_ref)
    m_ref[...] = jnp.zeros_like(m_ref)
    r = lax.broadcasted_iota(jnp.int32, (L, L), 0)
    cc = lax.broadcasted_iota(jnp.int32, (L, L), 1)
    causal = cc <= r
    neg = jnp.float32(-jnp.inf)

    def chunk_head(start, hh):
        cols = slice(hh * d, (hh + 1) * d)
        q = q_ref[0, pl.ds(start, L), cols]
        ks = k_ref[0, pl.ds(start, L), cols] * (1.0 / math.sqrt(d))
        v = v_ref[0, pl.ds(start, L), cols]
        gcol = gc_ref[0, hh, pl.ds(start, L), :]
        grow = gr_ref[0, hh, :, pl.ds(start, L)]
        i_col, i_row = gcol[:, 0:1], grow[0:1, :]
        lf_col, lf_row = _log_sigmoid(gcol[:, 1:2]), _log_sigmoid(grow[1:2, :])
        b_col = jnp.sum(jnp.where(causal, lf_row, 0.0), axis=-1, keepdims=True)
        b_row = jnp.sum(jnp.where(r <= cc, lf_col, 0.0), axis=0, keepdims=True)
        m_prev = m_ref[hh, 0:1, 0:1]
        dmat = jnp.where(causal, b_col - b_row + i_row, neg)
        m_t = jnp.maximum(b_col + m_prev, jnp.max(dmat, axis=-1, keepdims=True))
        w = jnp.exp(dmat - m_t)
        inter = jnp.exp(b_col + m_prev - m_t)
        qb = q.astype(BF16)
        s = lax.dot_general(qb, ks.astype(BF16), NT_DIMS, preferred_element_type=F32)
        sw = w * s
        c = c_ref[hh]
        n = n_ref[hh, 0:1, :]
        num = (jnp.dot(sw.astype(BF16), v.astype(BF16), preferred_element_type=F32)
               + inter * lax.dot_general(qb, c.astype(BF16), NT_DIMS, preferred_element_type=F32))
        den = jnp.sum(sw, axis=-1, keepdims=True) + inter * jnp.sum(q * n, axis=-1, keepdims=True)
        h = num / jnp.maximum(jnp.abs(den), jnp.exp(-m_t))
        m_new = m_t[L - 1:L, :]
        decay = inter[L - 1:L, :]
        w_last = jnp.exp(b_col[L - 1:L, :] - b_col + i_col - m_new)
        c_ref[hh] = decay * c + _dot3((v * w_last).T, ks)
        n_ref[hh] = jnp.broadcast_to(decay * n + jnp.sum(ks * w_last, axis=0, keepdims=True),
                                     n_ref.shape[1:])
        m_ref[hh] = jnp.broadcast_to(m_new, m_ref.shape[1:])
        y = h * _sigmoid(og_ref[0, pl.ds(start, L), cols])
        ms = jnp.mean(y * y, axis=-1, keepdims=True)
        y_ref[pl.ds(start, L), cols] = (y * lax.rsqrt(ms + EPS) * g_ref[:, cols]).astype(y_ref.dtype)

    def chunk(ci, _):
        start = pl.multiple_of(ci * L, L)
        for hh in range(n_hh):
            chunk_head(start, hh)
        return 0

    lax.fori_loop(0, nc, chunk, 0)
    c_out[0] = c_ref[...]
    n_out[0] = n_ref[...]
    m_out[0] = m_ref[...]


def mlstm_prompt(z3, gates, gain, B, T, *, L=128):
    H = z3.shape[2] // LANE
    d = LANE
    g4 = gates.reshape(B, T, 2, H)
    gcol = g4.transpose(0, 3, 1, 2)
    grow = g4.transpose(0, 3, 2, 1)
    n_hh = 2 if H % 2 == 0 else 1
    kern = functools.partial(_mlstm_prompt_kernel, L=L, nc=T // L, d=d, n_hh=n_hh)
    col = lambda grp: pl.BlockSpec((1, T, n_hh * d), lambda b, h: (grp, b, h))
    y, c, n, m = pl.pallas_call(
        kern, grid=(B, H // n_hh),
        in_specs=[col(8), col(9), col(10), col(11),
                  pl.BlockSpec((1, n_hh, T, 2), lambda b, h: (b, h, 0, 0)),
                  pl.BlockSpec((1, n_hh, 2, T), lambda b, h: (b, h, 0, 0)),
                  pl.BlockSpec((1, n_hh * d), lambda b, h: (0, h))],
        out_specs=[pl.BlockSpec((T, n_hh * d), lambda b, h: (b, h)),
                   pl.BlockSpec((1, n_hh, d, d), lambda b, h: (b, h, 0, 0)),
                   pl.BlockSpec((1, n_hh, 8, d), lambda b, h: (b, h, 0, 0)),
                   pl.BlockSpec((1, n_hh, 8, LANE), lambda b, h: (b, h, 0, 0))],
        out_shape=[jax.ShapeDtypeStruct((B * T, H * d), BF16),
                   jax.ShapeDtypeStruct((B, H, d, d), F32),
                   jax.ShapeDtypeStruct((B, H, 8, d), F32),
                   jax.ShapeDtypeStruct((B, H, 8, LANE), F32)],
        scratch_shapes=[pltpu.VMEM((n_hh, d, d), F32), pltpu.VMEM((n_hh, 8, d), F32),
                        pltpu.VMEM((n_hh, 8, LANE), F32)],
        compiler_params=_cp("parallel", "parallel"), name="mlstm_prompt")(
            z3, z3, z3, z3, gcol, grow, gain.reshape(1, -1))
    return y, c, n[:, :, 0, :], m[:, :, 0, 0]


def _mlstm_step_kernel(q_ref, k_ref, v_ref, og_ref, i_ref, f_ref, c_ref, n_ref, m_ref, g_ref,
                       y_ref, c_out, n_out, m_out, *, d):
    q, v = q_ref[0], v_ref[0]
    k = k_ref[0] * (1.0 / math.sqrt(d))
    i_pre, lf = i_ref[0], _log_sigmoid(f_ref[0])
    c, n, m = c_ref[0], n_ref[0], m_ref[0]
    m_t = jnp.maximum(lf + m, i_pre)
    w = jnp.exp(i_pre - m_t)
    inter = jnp.exp(lf + m - m_t)
    sw = w * jnp.sum(q * k, axis=-1, keepdims=True)
    eye = (lax.broadcasted_iota(jnp.int32, (1, d, d), 1) == lax.broadcasted_iota(jnp.int32, (1, d, d), 2))
    cq_col = jnp.sum(c * q, axis=-1, keepdims=True)
    cq = jnp.sum(jnp.where(eye, cq_col, 0.0), axis=1, keepdims=True)
    num = sw * v + inter * cq
    den = sw + inter * jnp.sum(n * q, axis=-1, keepdims=True)
    h = num / jnp.maximum(jnp.abs(den), jnp.exp(-m_t))
    v_col = jnp.sum(jnp.where(eye, v, 0.0), axis=-1, keepdims=True)
    c_out[0] = inter * c + (w * v_col) * k
    n_out[0] = inter * n + w * k
    m_out[0] = m_t
    y = h * _sigmoid(og_ref[0])
    ms = jnp.mean(y * y, axis=-1, keepdims=True)
    y_ref[0] = y * lax.rsqrt(ms + EPS) * g_ref[...]


def mlstm_step(q, k, v, og, i_pre, f_pre, c0, n0, m0, gain):
    G, H, d = n0.shape
    vec = lambda a: a.reshape(G, H, 1, d)
    sca = lambda a: a.reshape(G, H, 1, 1)
    vspec = pl.BlockSpec((1, H, 1, d), lambda b: (b, 0, 0, 0))
    sspec = pl.BlockSpec((1, H, 1, 1), lambda b: (b, 0, 0, 0))
    cspec = pl.BlockSpec((1, H, d, d), lambda b: (b, 0, 0, 0))
    kern = functools.partial(_mlstm_step_kernel, d=d)
    y, c, n, m = pl.pallas_call(
        kern, grid=(G,),
        in_specs=[vspec, vspec, vspec, vspec, sspec, sspec, cspec, vspec, sspec,
                  pl.BlockSpec((H, 1, d), lambda b: (0, 0, 0))],
        out_specs=[vspec, cspec, vspec, sspec],
        out_shape=[jax.ShapeDtypeStruct((G, H, 1, d), F32), jax.ShapeDtypeStruct((G, H, d, d), F32),
                   jax.ShapeDtypeStruct((G, H, 1, d), F32), jax.ShapeDtypeStruct((G, H, 1, 1), F32)],
        compiler_params=_cp("parallel"), name="mlstm_step")(
            vec(q), vec(k), vec(v), vec(og), sca(i_pre), sca(f_pre), c0, vec(n0), sca(m0),
            gain.reshape(H, 1, d))
    return y.reshape(G, H * d), c, n.reshape(G, H, d), m.reshape(G, H)


def _page_specs(P, n_pages, block):
    zeros = (0,) * (len(block) - 1)

    def spec(p):
        return pl.BlockSpec(block, lambda b, s, rows: (rows[b * n_pages + s * P + p],) + zeros)
    return [spec(p) for p in range(P)]


def _decode_scores_kernel(rows_ref, q_ref, *rest, P, page):
    pages, o_ref = rest[:P], rest[P]
    n_grp, d = q_ref.shape[1], q_ref.shape[2]
    q = jnp.broadcast_to(q_ref[0] * (1.0 / math.sqrt(d)), (n_grp, d, page))
    for p in range(P):
        o_ref[0, :, :, p * page:(p + 1) * page] = jnp.sum(pages[p][0] * q, axis=1, keepdims=True)


def decode_scores(q, pool_t, rows, n_pages, *, P=32):
    G = q.shape[0]
    _, n_grp, d, page = pool_t.shape
    P = _pick_tile(n_pages, P)
    kern = functools.partial(_decode_scores_kernel, P=P, page=page)
    grid_spec = pltpu.PrefetchScalarGridSpec(
        num_scalar_prefetch=1, grid=(G, n_pages // P),
        in_specs=[pl.BlockSpec((1, n_grp, d, 1), lambda b, s, rows: (b, 0, 0, 0))]
        + _page_specs(P, n_pages, (1, n_grp, d, page)),
        out_specs=pl.BlockSpec((1, n_grp, 1, P * page), lambda b, s, rows: (b, 0, 0, s)))
    return pl.pallas_call(
        kern, grid_spec=grid_spec,
        out_shape=jax.ShapeDtypeStruct((G, n_grp, 1, n_pages * page), F32),
        compiler_params=_cp("parallel", "arbitrary"), name="decode_scores")(
            rows, q.reshape(G, n_grp, d, 1), *([pool_t] * P))


def _sb_weights_kernel(z_ref, w_ref, *, n_tiles):
    bb = z_ref.shape[0]
    r = lax.broadcasted_iota(jnp.int32, (LANE, LANE), 0)
    c = lax.broadcasted_iota(jnp.int32, (LANE, LANE), 1)
    upper = jnp.where(r > c, 1.0, 0.0).astype(F32)
    carry = jnp.zeros((bb * 8, 1), F32)
    for t in reversed(range(n_tiles)):
        z = z_ref[:, :, t * LANE:(t + 1) * LANE].reshape(bb * 8, LANE)
        lk = -_softplus(z)
        la = jnp.dot(lk, upper, precision=HI, preferred_element_type=F32) + carry
        w_ref[:, :, t * LANE:(t + 1) * LANE] = jnp.exp(z + lk + la).reshape(bb, 8, LANE)
        carry = carry + jnp.sum(lk, axis=-1, keepdims=True)


def sb_weights(z, *, bb=8):
    G, _, Tp = z.shape
    bb = _pick_tile(G, bb)
    kern = functools.partial(_sb_weights_kernel, n_tiles=Tp // LANE)
    spec = pl.BlockSpec((bb, 8, Tp), lambda i: (i, 0, 0))
    return pl.pallas_call(kern, grid=(G // bb,), in_specs=[spec], out_specs=spec,
                          out_shape=jax.ShapeDtypeStruct(z.shape, F32),
                          compiler_params=_cp("parallel"), name="sb_weights")(z)


def _diff_weights_kernel(z_ref, q_ref, k_ref, v_ref, lv_ref, w_ref, new_ref, *, lam_init):
    bb, _, Tp = z_ref.shape
    width = q_ref.shape[-1]
    lam = (_diff_lambda(lv_ref) + lam_init).reshape(1, 1, 1)
    prod = q_ref[...] * k_ref[...] * (1.0 / math.sqrt(64.0))
    grp = lax.broadcasted_iota(jnp.int32, (1, 8, width), 2) // 64
    r8 = lax.broadcasted_iota(jnp.int32, (1, 8, width), 1)
    mine = grp == (r8 % 4) * 2 + r8 // 4
    s_new = jnp.sum(jnp.where(mine, prod, 0.0), axis=-1, keepdims=True)
    head = lax.broadcasted_iota(jnp.int32, (1, 8, 1), 1) % 4
    slope = jnp.exp((-2.0 * math.log(2.0)) * (head + 1).astype(F32))
    kpos = lax.broadcasted_iota(jnp.int32, (1, 1, Tp), 2)
    s = z_ref[...] - slope * (Tp - kpos).astype(F32)
    m = jnp.maximum(jnp.max(s, axis=-1, keepdims=True), s_new)
    p = jnp.exp(s - m)
    p_new = jnp.exp(s_new - m)
    den = jnp.sum(p, axis=-1, keepdims=True) + p_new
    p = p / den
    p_new = p_new / den
    w = p[:, 0:4, :] - lam * p[:, 4:8, :]
    w8 = jnp.concatenate([w, jnp.zeros_like(w)], axis=1).reshape(bb * 8, Tp)
    n_heads = 4
    spread = (lax.broadcasted_iota(jnp.int32, (LANE, n_heads * LANE), 1) // n_heads
              == lax.broadcasted_iota(jnp.int32, (LANE, n_heads * LANE), 0)).astype(F32)
    keep = (lax.broadcasted_iota(jnp.int32, (bb * 8, n_heads * LANE), 1) % n_heads
            == lax.broadcasted_iota(jnp.int32, (bb * 8, n_heads * LANE), 0) % 8)
    for t in range(Tp // LANE):
        wide = jnp.dot(w8[:, t * LANE:(t + 1) * LANE], spread, precision=HI,
                       preferred_element_type=F32)
        w_ref[:, :, t * n_heads * LANE:(t + 1) * n_heads * LANE] = jnp.where(keep, wide, 0.0).reshape(
            bb, 8, n_heads * LANE)
    w_new = p_new[:, 0:4, :] - lam * p_new[:, 4:8, :]
    new_ref[...] = w_new * v_ref[...]


def diff_weights(z, q, k_new, v_new, lam_vec, lam_init, *, bb=8):
    G, _, Tp = z.shape
    width = q.shape[-1]
    bb = _pick_tile(G, bb)
    kern = functools.partial(_diff_weights_kernel, lam_init=lam_init)
    zspec = pl.BlockSpec((bb, 8, Tp), lambda i: (i, 0, 0))
    qspec = pl.BlockSpec((bb, 1, width), lambda i: (i, 0, 0))
    vspec = pl.BlockSpec((bb, 4, LANE), lambda i: (i, 0, 0))
    r3 = lambda a: a.reshape(G, 1, width)
    return pl.pallas_call(
        kern, grid=(G // bb,),
        in_specs=[zspec, qspec, qspec, vspec, pl.BlockSpec(lam_vec.shape, lambda i: (0, 0))],
        out_specs=[pl.BlockSpec((bb, 8, 4 * Tp), lambda i: (i, 0, 0)), vspec],
        out_shape=[jax.ShapeDtypeStruct((G, 8, 4 * Tp), F32), jax.ShapeDtypeStruct((G, 4, LANE), F32)],
        compiler_params=_cp("parallel"), name="diff_weights")(
            z, r3(q), r3(k_new), v_new.reshape(G, 4, LANE), lam_vec)


def _decode_pv_t_kernel(rows_ref, w_ref, *rest, P, page):
    pages, o_ref, acc_ref = rest[:P], rest[P], rest[P + 1]
    s = pl.program_id(1)

    @pl.when(s == 0)
    def _():
        acc_ref[...] = jnp.zeros_like(acc_ref)

    acc = acc_ref[...]
    for p in range(P):
        acc = acc + pages[p][0] * w_ref[0, :, :, p * page:(p + 1) * page]
    acc_ref[...] = acc

    @pl.when(s == pl.num_programs(1) - 1)
    def _():
        d = acc.shape[1]
        col = jnp.sum(acc, axis=-1, keepdims=True)
        eye = (lax.broadcasted_iota(jnp.int32, (1, d, d), 1)
               == lax.broadcasted_iota(jnp.int32, (1, d, d), 2))
        o_ref[0] = jnp.sum(jnp.where(eye, col, 0.0), axis=1, keepdims=True)


def decode_pv_t(w, pool_t, rows, n_pages, *, P=32):
    G = w.shape[0]
    _, n_grp, d, page = pool_t.shape
    P = _pick_tile(n_pages, P)
    kern = functools.partial(_decode_pv_t_kernel, P=P, page=page)
    grid_spec = pltpu.PrefetchScalarGridSpec(
        num_scalar_prefetch=1, grid=(G, n_pages // P),
        in_specs=[pl.BlockSpec((1, n_grp, 1, P * page), lambda b, s, rows: (b, 0, 0, s))]
        + _page_specs(P, n_pages, (1, n_grp, d, page)),
        out_specs=pl.BlockSpec((1, n_grp, 1, d), lambda b, s, rows: (b, 0, 0, 0)),
        scratch_shapes=[pltpu.VMEM((n_grp, d, page), F32)])
    out = pl.pallas_call(
        kern, grid_spec=grid_spec, out_shape=jax.ShapeDtypeStruct((G, n_grp, 1, d), F32),
        compiler_params=_cp("parallel", "arbitrary"), name="decode_pv_t")(rows, w, *([pool_t] * P))
    return out.reshape(G, n_grp * d)


def _decode_pv_rows_kernel(rows_ref, w_ref, init_ref, *rest, P, n_rows):
    pages, o_ref, acc_ref = rest[:P], rest[P], rest[P + 1]
    s = pl.program_id(1)

    @pl.when(s == 0)
    def _():
        acc_ref[...] = jnp.zeros_like(acc_ref)

    acc = acc_ref[...]
    for p in range(P):
        acc = acc + _dot3(w_ref[0, :, p * n_rows:(p + 1) * n_rows], pages[p][0])
    acc_ref[...] = acc

    @pl.when(s == pl.num_programs(1) - 1)
    def _():
        o_ref[0] = acc[0:4] + init_ref[0]


def decode_pv_rows(w, pool, rows, init, n_pages, *, P=32):
    G = w.shape[0]
    _, n_rows, width = pool.shape
    P = _pick_tile(n_pages, P)
    kern = functools.partial(_decode_pv_rows_kernel, P=P, n_rows=n_rows)
    grid_spec = pltpu.PrefetchScalarGridSpec(
        num_scalar_prefetch=1, grid=(G, n_pages // P),
        in_specs=[pl.BlockSpec((1, 8, P * n_rows), lambda b, s, rows: (b, 0, s)),
                  pl.BlockSpec((1, 4, width), lambda b, s, rows: (b, 0, 0))]
        + _page_specs(P, n_pages, (1, n_rows, width)),
        out_specs=pl.BlockSpec((1, 4, width), lambda b, s, rows: (b, 0, 0)),
        scratch_shapes=[pltpu.VMEM((8, width), F32)])
    out = pl.pallas_call(
        kern, grid_spec=grid_spec, out_shape=jax.ShapeDtypeStruct((G, 4, width), F32),
        compiler_params=_cp("parallel", "arbitrary"), name="decode_pv_rows")(
            rows, w, init, *([pool] * P))
    return out.reshape(G, 4 * width)


def _group_norm_kernel(x_ref, g_ref, o_ref, *, group, post_scale):
    x = x_ref[...]
    ms = jnp.dot(x * x, _group_mean_matrix(x.shape[-1], group), precision=HI,
                 preferred_element_type=F32)
    o_ref[...] = x * lax.rsqrt(ms + EPS) * g_ref[...] * post_scale


def _moe_up_kernel(te_ref, act_ref, first_ref, a_ref, wg_ref, wu_ref, o_ref, wg_bf, wu_bf):
    i = pl.program_id(1)

    @pl.when(first_ref[i] > 0)
    def _():
        wg_bf[...] = wg_ref[...].astype(BF16)
        wu_bf[...] = wu_ref[...].astype(BF16)

    @pl.when(act_ref[i] > 0)
    def _():
        a = a_ref[...].astype(BF16)
        g = jnp.dot(a, wg_bf[...], preferred_element_type=F32)
        u = jnp.dot(a, wu_bf[...], preferred_element_type=F32)
        o_ref[...] = (g * _sigmoid(g) * u).astype(o_ref.dtype)

    @pl.when(act_ref[i] == 0)
    def _():
        o_ref[...] = jnp.zeros_like(o_ref)


def _moe_down_kernel(te_ref, act_ref, first_ref, a_ref, wd_ref, o_ref, wd_bf):
    i = pl.program_id(1)

    @pl.when(first_ref[i] > 0)
    def _():
        wd_bf[...] = wd_ref[...].astype(BF16)

    @pl.when(act_ref[i] > 0)
    def _():
        o_ref[...] = jnp.dot(a_ref[...], wd_bf[...], preferred_element_type=F32)

    @pl.when(act_ref[i] == 0)
    def _():
        o_ref[...] = jnp.zeros_like(o_ref)


def moe_grouped(xg, tile_expert, tile_active, tile_first, wg, wu, wd, *, tg, tn_up=1408, tn_down=1024):
    n_rows, D = xg.shape
    F = wg.shape[-1]
    n_tiles = n_rows // tg
    tn_up, tn_down = _pick_tile(F, tn_up), _pick_tile(D, tn_down)
    once = pl.Buffered(1)
    up_spec = pltpu.PrefetchScalarGridSpec(
        num_scalar_prefetch=3, grid=(F // tn_up, n_tiles),
        in_specs=[pl.BlockSpec((tg, D), lambda j, i, te, ac, fi: (i, 0)),
                  pl.BlockSpec((None, D, tn_up), lambda j, i, te, ac, fi: (te[i], 0, j), pipeline_mode=once),
                  pl.BlockSpec((None, D, tn_up), lambda j, i, te, ac, fi: (te[i], 0, j), pipeline_mode=once)],
        out_specs=pl.BlockSpec((tg, tn_up), lambda j, i, te, ac, fi: (i, j)),
        scratch_shapes=[pltpu.VMEM((D, tn_up), BF16), pltpu.VMEM((D, tn_up), BF16)])
    act = pl.pallas_call(_moe_up_kernel, grid_spec=up_spec,
                         out_shape=jax.ShapeDtypeStruct((n_rows, F), BF16),
                         compiler_params=_cp("arbitrary", "arbitrary"), name="moe_up")(
                             tile_expert, tile_active, tile_first, xg, wg, wu)
    down_spec = pltpu.PrefetchScalarGridSpec(
        num_scalar_prefetch=3, grid=(D // tn_down, n_tiles),
        in_specs=[pl.BlockSpec((tg, F), lambda j, i, te, ac, fi: (i, 0)),
                  pl.BlockSpec((None, F, tn_down), lambda j, i, te, ac, fi: (te[i], 0, j))],
        out_specs=pl.BlockSpec((tg, tn_down), lambda j, i, te, ac, fi: (i, j)),
        scratch_shapes=[pltpu.VMEM((F, tn_down), BF16)])
    return pl.pallas_call(_moe_down_kernel, grid_spec=down_spec,
                          out_shape=jax.ShapeDtypeStruct((n_rows, D), F32),
                          compiler_params=_cp("arbitrary", "arbitrary"), name="moe_down")(
                              tile_expert, tile_active, tile_first, act, wd)


def _moe_dispatch(idx, n_experts, tg):
    M = idx.shape[0]
    flat_e = idx.reshape(-1)
    onehot = (flat_e[:, None] == jnp.arange(n_experts, dtype=jnp.int32)[None, :]).astype(jnp.int32)
    counts = jnp.sum(onehot, axis=0)
    rank = jnp.sum((jnp.cumsum(onehot, axis=0) - onehot) * onehot, axis=1)
    padded = (counts + tg - 1) // tg * tg
    ends = jnp.cumsum(padded)
    pos = (ends - padded)[flat_e] + rank
    n_rows = 2 * M + n_experts * tg
    row_token = jnp.zeros((n_rows,), jnp.int32).at[pos].set(jnp.arange(2 * M, dtype=jnp.int32) // 2)
    tile_start = jnp.arange(n_rows // tg, dtype=jnp.int32) * tg
    tile_expert = jnp.minimum(jnp.sum((tile_start[:, None] >= ends[None, :]).astype(jnp.int32), axis=1),
                              n_experts - 1)
    tile_expert = tile_expert.astype(jnp.int32)
    tile_active = (tile_start < ends[-1]).astype(jnp.int32)
    tile_first = jnp.concatenate([jnp.ones((1,), jnp.int32),
                                  (tile_expert[1:] != tile_expert[:-1]).astype(jnp.int32)])
    return pos.reshape(M, 2), row_token, tile_expert, tile_active, tile_first


def _moe_combine_kernel(x_ref, g_ref, y1_ref, y2_ref, w_ref, o_ref):
    w = w_ref[...]
    f = w[:, 0:1] * y1_ref[...] + w[:, 1:2] * y2_ref[...]
    o_ref[...] = x_ref[...] + g_ref[0] * f


def moe_combine(x, g, y1, y2, wts, rows_per_gate, *, tm=512):
    M, D = x.shape
    tm = _pick_tile(rows_per_gate, tm)
    rpt = rows_per_gate // tm
    row = pl.BlockSpec((tm, D), lambda i: (i, 0))
    return pl.pallas_call(
        _moe_combine_kernel, grid=(M // tm,),
        in_specs=[row, pl.BlockSpec((1, 1, D), lambda i: (i // rpt, 0, 0)), row, row,
                  pl.BlockSpec((tm, LANE), lambda i: (i, 0))],
        out_specs=row, out_shape=jax.ShapeDtypeStruct((M, D), F32),
        compiler_params=_cp("parallel"), name="moe_combine")(x, g.reshape(-1, 1, D), y1, y2, wts)


def _moe_dense_up_kernel(a_ref, wg_ref, wu_ref, o_ref, accg_ref, accu_ref, *, nk):
    k = pl.program_id(2)
    a = a_ref[...]
    pg = _dot3(a, wg_ref[...])
    pu = _dot3(a, wu_ref[...])

    @pl.when(k == 0)
    def _():
        accg_ref[...] = pg
        accu_ref[...] = pu

    @pl.when(k > 0)
    def _():
        accg_ref[...] += pg
        accu_ref[...] += pu

    @pl.when(k == nk - 1)
    def _():
        g = accg_ref[...]
        o_ref[...] = g * _sigmoid(g) * accu_ref[...]


def _moe_dense_down_kernel(a_ref, wd_ref, cw_ref, x_ref, g_ref, o_ref, acc_ref):
    e, k = pl.program_id(0), pl.program_id(1)
    part = cw_ref[...] * _dot3(a_ref[...], wd_ref[...])

    @pl.when((e == 0) & (k == 0))
    def _():
        acc_ref[...] = part

    @pl.when((e > 0) | (k > 0))
    def _():
        acc_ref[...] += part

    @pl.when((e == pl.num_programs(0) - 1) & (k == pl.num_programs(1) - 1))
    def _():
        o_ref[...] = x_ref[...] + g_ref[...] * acc_ref[...]


def moe_dense(h, x, gate, combine, wg, wu, wd, *, tn=1408, tk=1024, tkd=1408):
    M, D = h.shape
    E, _, F = wg.shape
    tn, tk, tkd = _pick_tile(F, tn), _pick_tile(D, tk), _pick_tile(F, tkd)
    nk = D // tk
    wspec = pl.BlockSpec((None, tk, tn), lambda e, j, k: (e, k, j))
    act = pl.pallas_call(
        functools.partial(_moe_dense_up_kernel, nk=nk), grid=(E, F // tn, nk),
        in_specs=[pl.BlockSpec((M, tk), lambda e, j, k: (0, k)), wspec, wspec],
        out_specs=pl.BlockSpec((None, M, tn), lambda e, j, k: (e, 0, j)),
        out_shape=jax.ShapeDtypeStruct((E, M, F), F32),
        scratch_shapes=[pltpu.VMEM((M, tn), F32), pltpu.VMEM((M, tn), F32)],
        compiler_params=_cp("parallel", "parallel", "arbitrary"), name="moe_dense_up")(h, wg, wu)
    full = pl.BlockSpec((M, D), lambda e, k: (0, 0))
    return pl.pallas_call(
        _moe_dense_down_kernel, grid=(E, F // tkd),
        in_specs=[pl.BlockSpec((None, M, tkd), lambda e, k: (e, 0, k)),
                  pl.BlockSpec((None, tkd, D), lambda e, k: (e, k, 0)),
                  pl.BlockSpec((None, M, 1), lambda e, k: (e, 0, 0)), full, full],
        out_specs=full, out_shape=jax.ShapeDtypeStruct((M, D), F32),
        scratch_shapes=[pltpu.VMEM((M, D), F32)],
        compiler_params=_cp("arbitrary", "arbitrary"), name="moe_dense_down")(
            act, wd, combine, x, gate)


def group_norm(x, gain, *, group, post_scale=1.0):
    kern = functools.partial(_group_norm_kernel, group=group, post_scale=post_scale)
    return pl.pallas_call(kern, out_shape=jax.ShapeDtypeStruct(x.shape, F32), name="group_norm")(
        x, gain.reshape(1, -1))


W_GROUP = 512
N_MAIN = 12 * W_GROUP
MOE_TILE = 256


def _lam_init(l):
    return 0.8 - 0.6 * math.exp(-0.3 * l)


def _gate_weight(w_in, l):
    wg = w_in.swapaxes(1, 2)[l, N_MAIN:, :].T
    return jnp.zeros((wg.shape[0], LANE), F32).at[:, :wg.shape[1]].set(wg)


def _prompt_trunk(x3, mods, p, wb):
    B, T, D = x3.shape
    M = B * T
    x = x3.reshape(M, D)
    news = []
    for l in range(len(mods)):
        sh1, sc1, g1, sh2, sc2, g2 = mods[l]
        h = norm_mod(x, p['g_norm1'][l], sc1, sh1, out_dtype=BF16, rows_per_mod=T, name="norm1_p")
        z3, sbk_t, sbv_t, dfk_t = w_in_prompt(h, wb['w_in'], l, B, T, n_groups=12, width=W_GROUP,
                                              kt_groups=(1, 2, 4))
        gates = matmul(h, _gate_weight(p['w_in'], l).astype(BF16), precise=False, name="gates_p")
        gates = gates[:, :8] + jnp.concatenate([p['ml_b_i'][l], p['ml_b_f'][l]])[None, :]
        y_sb = sb_attention_prompt(z3, p['g_sb_out'][l], B, T)
        y_df = diff_attention_prompt(z3, p['g_diff_out'][l], p['diff_lam'][l], _lam_init(l), B, T)
        y_lru, conv_new, h_new = lru_prompt(z3, p, l, B, T)
        y_ml, c_new, n_new, m_new = mlstm_prompt(z3, gates, p['g_ml_out'][l], B, T)
        ycat = jnp.concatenate([y_sb, y_df, y_lru, y_ml], axis=-1)
        x = matmul(ycat, wb['w_out'], precise=False, layer=l, resid=(x, g1), rows_per_gate=T,
                   name="w_out_p")
        j = l // 2
        if l % 2 == 0:
            h2 = norm_mod(x, p['g_norm2'][l], sc2, sh2, out_dtype=BF16, rows_per_mod=T, name="norm2_p")
            act = swiglu_up(h2, wb['ffn_w_gate'], wb['ffn_w_up'], precise=False, out_dtype=BF16,
                            layer=j, name="ffn_up_p")
            x = matmul(act, wb['ffn_w_down'], precise=False, layer=j, tk=2816, resid=(x, g2),
                       rows_per_gate=T, name="ffn_down_p")
        else:
            h2, idx, wts = norm_mod(x, p['g_norm2'][l], sc2, sh2, out_dtype=F32, rows_per_mod=T,
                                    router=(p['moe_w_router'][j], p['moe_b_router'][j]),
                                    name="norm2_router_p")
            n_experts = p['moe_w_router'].shape[-1]
            pos, row_token, tile_expert, tile_active, tile_first = _moe_dispatch(
                idx[:, :2], n_experts, MOE_TILE)
            take = lambda a, rows: a.at[rows].get(mode="promise_in_bounds")
            xg = take(h2, row_token)
            yg = moe_grouped(xg, tile_expert, tile_active, tile_first, p['moe_w_gate'][j],
                             p['moe_w_up'][j], p['moe_w_down'][j], tg=MOE_TILE)
            x = moe_combine(x, g2, take(yg, pos[:, 0]), take(yg, pos[:, 1]), wts, T)
        news.append((sbk_t.reshape(B, 8, 64, T).transpose(0, 3, 1, 2),
                     sbv_t.reshape(B, 8, 64, T).transpose(0, 3, 1, 2),
                     dfk_t.reshape(B, 4, 2, 64, T).transpose(0, 4, 1, 2, 3),
                     z3[5].reshape(B, T, 4, 128), conv_new, h_new, c_new, n_new, m_new))
    y = norm_mod(x, p['g_final'], out_dtype=F32, name="final_norm_p").reshape(B, T, D)
    return y, tuple(jnp.stack([nw[i] for nw in news], axis=0) for i in range(9))


def _sample_trunk(x3, mods, p, caches, states, page_table):
    G, _, D = x3.shape
    x = x3.reshape(G, D)
    sb_k_c, sb_v_c, df_k_c, df_v_c = caches
    n_layers, n_phys, page = sb_k_c.shape[:3]
    n_pool = n_layers * n_phys
    pool_sb_k = sb_k_c.transpose(0, 1, 3, 4, 2).reshape(n_pool, 8, 64, page)
    pool_sb_v = sb_v_c.transpose(0, 1, 3, 4, 2).reshape(n_pool, 8, 64, page)
    pool_df_k = df_k_c.transpose(0, 1, 3, 4, 5, 2).reshape(n_pool, 8, 64, page)
    pool_df_v = df_v_c.reshape(n_pool, page * 4, LANE)
    n_pages = page_table.shape[1]
    conv_all, h_all, c_all, n_all, m_all = states
    news = []
    for l in range(len(mods)):
        sh1, sc1, g1, sh2, sc2, g2 = mods[l]
        rows = (page_table + l * n_phys).reshape(-1).astype(jnp.int32)
        h = norm_mod(x, p['g_norm1'][l], sc1, sh1, out_dtype=F32, name="norm1_s")
        z = matmul(h, p['w_in'].swapaxes(1, 2), precise=True, layer=l, n_cols=N_MAIN, tn=W_GROUP,
                   b_transposed=True, name="w_in_s")
        gates = matmul(h, _gate_weight(p['w_in'], l), precise=True, name="gates_s")
        zs = [z[:, i * W_GROUP:(i + 1) * W_GROUP] for i in range(12)]
        sb_q, sb_k, sb_v, df_q, df_k, df_v, lru_x, lru_g, ml_q, ml_k, ml_v, ml_o = zs
        ml_i = gates[:, 0:4] + p['ml_b_i'][l][None, :]
        ml_f = gates[:, 4:8] + p['ml_b_f'][l][None, :]
        Tp = n_pages * page
        z_sb = decode_scores(sb_q, pool_sb_k, rows, n_pages).reshape(G, 8, Tp)
        w_sb = sb_weights(z_sb).reshape(G, 8, 1, Tp)
        y_sb = decode_pv_t(w_sb, pool_sb_v, rows, n_pages)
        y_sb = group_norm(y_sb, p['g_sb_out'][l], group=64)
        z_df = decode_scores(df_q, pool_df_k, rows, n_pages)
        z_df = z_df.reshape(G, 4, 2, Tp).swapaxes(1, 2).reshape(G, 8, Tp)
        w_df, new_df = diff_weights(z_df, df_q, df_k, df_v, p['diff_lam'][l], _lam_init(l))
        y_df = decode_pv_rows(w_df, pool_df_v, rows, new_df, n_pages)
        y_df = group_norm(y_df, p['g_diff_out'][l], group=128, post_scale=1.0 - _lam_init(l))
        y_lru, conv_new, h_new = lru_step(lru_x, lru_g, conv_all[l], h_all[l], p, l)
        y_ml, c_new, n_new, m_new = mlstm_step(ml_q, ml_k, ml_v, ml_o, ml_i, ml_f, c_all[l],
                                               n_all[l], m_all[l], p['g_ml_out'][l])
        ycat = jnp.concatenate([y_sb, y_df, y_lru, y_ml], axis=-1)
        x = matmul(ycat, p['w_out'], precise=True, layer=l, resid=(x, g1), name="w_out_s")
        j = l // 2
        if l % 2 == 0:
            h2 = norm_mod(x, p['g_norm2'][l], sc2, sh2, out_dtype=F32, name="norm2_s")
            act = swiglu_up(h2, p['ffn_w_gate'], p['ffn_w_up'], precise=True, out_dtype=F32,
                            layer=j, tk=1024, name="ffn_up_s")
            x = matmul(act, p['ffn_w_down'], precise=True, layer=j, tk=1408, resid=(x, g2),
                       name="ffn_down_s")
        else:
            h2, idx, wts = norm_mod(x, p['g_norm2'][l], sc2, sh2, out_dtype=F32,
                                    router=(p['moe_w_router'][j], p['moe_b_router'][j]),
                                    name="norm2_router_s")
            n_experts = p['moe_w_router'].shape[-1]
            e_ids = jnp.arange(n_experts, dtype=jnp.int32)[:, None]
            combine = (jnp.where(idx[None, :, 0] == e_ids, wts[None, :, 0], 0.0)
                       + jnp.where(idx[None, :, 1] == e_ids, wts[None, :, 1], 0.0))[..., None]
            x = moe_dense(h2, x, g2, combine, p['moe_w_gate'][j], p['moe_w_up'][j], p['moe_w_down'][j])
        news.append((sb_k.reshape(G, 1, 8, 64), sb_v.reshape(G, 1, 8, 64),
                     df_k.reshape(G, 1, 4, 2, 64), df_v.reshape(G, 1, 4, 128),
                     conv_new, h_new, c_new, n_new, m_new))
    y = norm_mod(x, p['g_final'], out_dtype=F32, name="final_norm_s").reshape(G, 1, D)
    return y, tuple(jnp.stack([nw[i] for nw in news], axis=0) for i in range(9))


def kernel(x_prompt, x_sample, cache_sb_k, cache_sb_v, cache_diff_k, cache_diff_v,
           state_lru_conv, state_lru_h, state_mlstm_c, state_mlstm_n, state_mlstm_m,
           page_table, c_prompt, c_sample, w_ada, b_ada, g_norm1, g_norm2, w_in, w_out,
           g_sb_out, diff_lam, g_diff_out, lru_conv_w, lru_conv_b, lru_w_r, lru_b_r,
           lru_w_i, lru_b_i, lru_lam, g_lru_out, ml_b_i, ml_b_f, g_ml_out,
           ffn_w_gate, ffn_w_up, ffn_w_down, moe_w_router, moe_b_router,
           moe_w_gate, moe_w_up, moe_w_down, g_final):
    p = dict(w_ada=w_ada, b_ada=b_ada, g_norm1=g_norm1, g_norm2=g_norm2, w_in=w_in, w_out=w_out,
             g_sb_out=g_sb_out, diff_lam=diff_lam, g_diff_out=g_diff_out, lru_conv_w=lru_conv_w,
             lru_conv_b=lru_conv_b, lru_w_r=lru_w_r, lru_b_r=lru_b_r, lru_w_i=lru_w_i,
             lru_b_i=lru_b_i, lru_lam=lru_lam, g_lru_out=g_lru_out, ml_b_i=ml_b_i, ml_b_f=ml_b_f,
             g_ml_out=g_ml_out, ffn_w_gate=ffn_w_gate, ffn_w_up=ffn_w_up, ffn_w_down=ffn_w_down,
             moe_w_router=moe_w_router, moe_b_router=moe_b_router, moe_w_gate=moe_w_gate,
             moe_w_up=moe_w_up, moe_w_down=moe_w_down, g_final=g_final)
    depth, D = g_norm1.shape
    Bp, Gs = x_prompt.shape[0], x_sample.shape[0]
    n_c = Bp + Gs
    c_all = jnp.zeros(((n_c + 15) // 16 * 16, D), F32).at[:Bp].set(c_prompt).at[Bp:n_c].set(c_sample)
    mods_p, mods_s = [], []
    for l in range(depth):
        m = matmul(c_all, w_ada, precise=True, layer=l, bias=b_ada[l], silu_a=True, name="ada")
        mods_p.append([m[:Bp, i * D:(i + 1) * D] for i in range(6)])
        mods_s.append([m[Bp:n_c, i * D:(i + 1) * D] for i in range(6)])
    wb = {k: p[k].astype(BF16) for k in ('w_in', 'w_out', 'ffn_w_gate', 'ffn_w_up', 'ffn_w_down')}
    y_p, new_p = _prompt_trunk(x_prompt, mods_p, p, wb)
    y_s, new_s = _sample_trunk(x_sample, mods_s, p,
                               (cache_sb_k, cache_sb_v, cache_diff_k, cache_diff_v),
                               (state_lru_conv, state_lru_h, state_mlstm_c, state_mlstm_n,
                                state_mlstm_m), page_table)
    return (y_p, y_s) + new_p + new_s
```

```python
import functools
import math

import jax
import jax.numpy as jnp
from jax import lax
from jax.experimental import pallas as pl
from jax.experimental.pallas import tpu as pltpu

F32 = jnp.float32
BF16 = jnp.bfloat16
HI = lax.Precision.HIGHEST
EPS = 1e-6
LRU_C = 8.0
LANE = 128
VMEM_LIMIT = 56 * 1024 * 1024

NT_DIMS = (((1,), (1,)), ((), ()))
F32_EXP_ZERO = -104.0


def _cp(*sem):
    return pltpu.CompilerParams(dimension_semantics=sem, vmem_limit_bytes=VMEM_LIMIT)


def _split(x):
    hi = x.astype(BF16)
    return hi, (x - hi.astype(F32)).astype(BF16)


def _dot3(a, b, dims=(((1,), (0,)), ((), ()))):
    m = a.shape[0]
    a_hi = a.astype(BF16)
    a_parts = jnp.concatenate([a, a - a_hi.astype(F32)], axis=0).astype(BF16)
    b_hi, b_lo = _split(b)
    r = lax.dot_general(a_parts, b_hi, dims, preferred_element_type=F32)
    return r[:m] + r[m:] + lax.dot_general(a_hi, b_lo, dims, preferred_element_type=F32)


def _dot(a, b, precise):
    if precise:
        return _dot3(a, b)
    return jnp.dot(a.astype(BF16), b.astype(BF16), preferred_element_type=F32)


def _softplus(x):
    return jnp.maximum(x, 0.0) + jnp.log1p(jnp.exp(-jnp.abs(x)))


def _sigmoid(x):
    return 1.0 / (1.0 + jnp.exp(-x))


def _group_mean_matrix(width, group):
    r = lax.broadcasted_iota(jnp.int32, (width, width), 0) // group
    c = lax.broadcasted_iota(jnp.int32, (width, width), 1) // group
    return jnp.where(r == c, 1.0 / group, 0.0).astype(F32)


def _pick_tile(n, pref):
    t = min(n, pref)
    while n % t:
        t //= 2
    return t


def _mm_kernel(*refs, nk, precise, has_bias, silu_a, resid, b_transposed):
    a_ref, b_ref = refs[0], refs[1]
    i = 2
    bias_ref = x_ref = g_ref = None
    if has_bias:
        bias_ref = refs[i]
        i += 1
    if resid:
        x_ref, g_ref = refs[i], refs[i + 1]
        i += 2
    o_ref, acc_ref = refs[i], refs[i + 1]
    k = pl.program_id(2)
    a = a_ref[...]
    if silu_a:
        a = a * _sigmoid(a)
    if b_transposed:
        assert precise
        part = _dot3(a, b_ref[...], NT_DIMS)
    else:
        part = _dot(a, b_ref[...], precise)

    def finish(acc):
        if has_bias:
            acc = acc + bias_ref[...]
        if resid:
            g = g_ref[...].reshape(-1, acc.shape[-1])
            acc = x_ref[...] + g * acc
        o_ref[...] = acc.reshape(o_ref.shape).astype(o_ref.dtype)

    if nk == 1:
        finish(part)
    else:
        @pl.when(k == 0)
        def _():
            acc_ref[...] = part

        @pl.when(k > 0)
        def _():
            acc_ref[...] += part

        @pl.when(k == nk - 1)
        def _():
            finish(acc_ref[...])


def _weight_spec(b, tk, tn, layer, transposed=False):
    if transposed:
        return pl.BlockSpec((None, tn, tk), lambda i, j, k: (layer, j, k))
    if b.ndim == 2:
        return pl.BlockSpec((tk, tn), lambda i, j, k: (k, j))
    return pl.BlockSpec((None, tk, tn), lambda i, j, k: (layer, k, j))


def matmul(a, b, *, precise, out_dtype=F32, tm=1024, tn=1024, tk=2048, bias=None, silu_a=False,
           out3d_width=None, resid=None, rows_per_gate=None, layer=0, n_cols=None,
           b_transposed=False, name="mm"):
    M, K = a.shape
    N = n_cols if n_cols is not None else (b.shape[-2] if b_transposed else b.shape[-1])
    tm, tn, tk = _pick_tile(M, tm), _pick_tile(N, tn), _pick_tile(K, tk)
    if rows_per_gate is not None:
        tm = _pick_tile(rows_per_gate, tm)
    nk = K // tk
    in_specs = [pl.BlockSpec((tm, tk), lambda i, j, k: (i, k)),
                _weight_spec(b, tk, tn, layer, b_transposed)]
    args = [a, b]
    if bias is not None:
        in_specs.append(pl.BlockSpec((1, tn), lambda i, j, k: (0, j)))
        args.append(bias.reshape(1, N))
    if resid is not None:
        x, g = resid
        in_specs.append(pl.BlockSpec((tm, tn), lambda i, j, k: (i, j)))
        args.append(x)
        if rows_per_gate is None:
            in_specs.append(pl.BlockSpec((tm, tn), lambda i, j, k: (i, j)))
            args.append(g)
        else:
            assert rows_per_gate % tm == 0
            rpt = rows_per_gate // tm
            in_specs.append(pl.BlockSpec((1, 1, tn), lambda i, j, k: (i // rpt, 0, j)))
            args.append(g.reshape(g.shape[0], 1, N))
    if out3d_width is None:
        out_shape = jax.ShapeDtypeStruct((M, N), out_dtype)
        out_spec = pl.BlockSpec((tm, tn), lambda i, j, k: (i, j))
    else:
        assert tn == out3d_width
        out_shape = jax.ShapeDtypeStruct((N // tn, M, tn), out_dtype)
        out_spec = pl.BlockSpec((1, tm, tn), lambda i, j, k: (j, i, 0))
    kern = functools.partial(_mm_kernel, nk=nk, precise=precise, has_bias=bias is not None,
                             silu_a=silu_a, resid=resid is not None, b_transposed=b_transposed)
    return pl.pallas_call(
        kern, grid=(M // tm, N // tn, nk), in_specs=in_specs, out_specs=out_spec,
        out_shape=out_shape, scratch_shapes=[pltpu.VMEM((tm, tn), F32)],
        compiler_params=_cp("parallel", "parallel", "arbitrary"), name=name)(*args)


def _w_in_prompt_kernel(a_ref, b_ref, z_ref, *kt_refs, kt_groups):
    res = jnp.dot(a_ref[...], b_ref[...], preferred_element_type=F32)
    z_ref[0] = res
    j = pl.program_id(1)
    for grp, ref in zip(kt_groups, kt_refs):
        @pl.when(j == grp)
        def _(ref=ref):
            ref[0] = res.T


def w_in_prompt(h, w, layer, B, T, *, n_groups, width, kt_groups, tm=1024):
    M, K = h.shape
    tm = _pick_tile(T, tm)
    per_seq = T // tm
    kern = functools.partial(_w_in_prompt_kernel, kt_groups=kt_groups)
    kt_spec = pl.BlockSpec((1, width, tm), lambda i, j: (i // per_seq, 0, i % per_seq))
    return pl.pallas_call(
        kern, grid=(M // tm, n_groups),
        in_specs=[pl.BlockSpec((tm, K), lambda i, j: (i, 0)),
                  pl.BlockSpec((None, K, width), lambda i, j: (layer, 0, j))],
        out_specs=[pl.BlockSpec((1, tm, width), lambda i, j: (j, i, 0))] + [kt_spec] * len(kt_groups),
        out_shape=[jax.ShapeDtypeStruct((n_groups, M, width), F32)]
        + [jax.ShapeDtypeStruct((B, width, T), F32)] * len(kt_groups),
        compiler_params=_cp("parallel", "arbitrary"), name="w_in_p")(h, w)


def _swiglu_kernel(a_ref, wg_ref, wu_ref, o_ref, accg_ref, accu_ref, *, nk, precise):
    k = pl.program_id(2)
    a = a_ref[...]
    pg = _dot(a, wg_ref[...], precise)
    pu = _dot(a, wu_ref[...], precise)

    def finish(g, u):
        o_ref[...] = (g * _sigmoid(g) * u).astype(o_ref.dtype)

    if nk == 1:
        finish(pg, pu)
    else:
        @pl.when(k == 0)
        def _():
            accg_ref[...] = pg
            accu_ref[...] = pu

        @pl.when(k > 0)
        def _():
            accg_ref[...] += pg
            accu_ref[...] += pu

        @pl.when(k == nk - 1)
        def _():
            finish(accg_ref[...], accu_ref[...])


def swiglu_up(a, wg, wu, *, precise, out_dtype, tm=1024, tn=512, tk=2048, layer=0, name="swiglu_up"):
    M, K = a.shape
    N = wg.shape[-1]
    tm, tn, tk = _pick_tile(M, tm), _pick_tile(N, tn), _pick_tile(K, tk)
    nk = K // tk
    kern = functools.partial(_swiglu_kernel, nk=nk, precise=precise)
    return pl.pallas_call(
        kern, grid=(M // tm, N // tn, nk),
        in_specs=[pl.BlockSpec((tm, tk), lambda i, j, k: (i, k)),
                  _weight_spec(wg, tk, tn, layer), _weight_spec(wu, tk, tn, layer)],
        out_specs=pl.BlockSpec((tm, tn), lambda i, j, k: (i, j)),
        out_shape=jax.ShapeDtypeStruct((M, N), out_dtype),
        scratch_shapes=[pltpu.VMEM((tm, tn), F32), pltpu.VMEM((tm, tn), F32)],
        compiler_params=_cp("parallel", "parallel", "arbitrary"), name=name)(a, wg, wu)


def _norm_kernel(*refs, modulated, router, n_experts):
    x_ref, g_ref = refs[0], refs[1]
    i = 2
    if modulated:
        sc_ref, sh_ref = refs[i], refs[i + 1]
        i += 2
    if router:
        wr_ref, br_ref = refs[i], refs[i + 1]
        i += 2
    o_ref = refs[i]
    x = x_ref[...]
    d = x.shape[-1]
    y = x * lax.rsqrt(jnp.mean(x * x, axis=-1, keepdims=True) + EPS) * g_ref[...]
    if modulated:
        sc = sc_ref[...].reshape(-1, d)
        sh = sh_ref[...].reshape(-1, d)
        y = y * (1.0 + sc) + sh
    o_ref[...] = y.astype(o_ref.dtype)
    if router:
        idx_ref, wt_ref = refs[i + 1], refs[i + 2]
        logits = jnp.dot(y, wr_ref[...], precision=HI, preferred_element_type=F32) + br_ref[...]
        lane = lax.broadcasted_iota(jnp.int32, logits.shape, 1)
        neg = jnp.float32(-jnp.inf)
        logits = jnp.where(lane < n_experts, logits, neg)
        m1 = jnp.max(logits, axis=-1, keepdims=True)
        i1 = jnp.min(jnp.where(logits == m1, lane, LANE), axis=-1, keepdims=True)
        rest = jnp.where(lane == i1, neg, logits)
        m2 = jnp.max(rest, axis=-1, keepdims=True)
        i2 = jnp.min(jnp.where(rest == m2, lane, LANE), axis=-1, keepdims=True)
        e2 = jnp.exp(m2 - m1)
        w1 = 1.0 / (1.0 + e2)
        w2 = e2 / (1.0 + e2)
        idx_ref[...] = jnp.where(lane == 0, i1, jnp.where(lane == 1, i2, 0))
        wt_ref[...] = jnp.where(lane == 0, w1, jnp.where(lane == 1, w2, 0.0))


def norm_mod(x, g, sc=None, sh=None, *, out_dtype, rows_per_mod=None, router=None, tm=512,
             name="norm"):
    M, D = x.shape
    tm = _pick_tile(M if rows_per_mod is None else rows_per_mod, tm)
    modulated = sc is not None
    in_specs = [pl.BlockSpec((tm, D), lambda i: (i, 0)), pl.BlockSpec((1, D), lambda i: (0, 0))]
    args = [x, g.reshape(1, D)]
    if modulated:
        if rows_per_mod is None:
            spec = pl.BlockSpec((tm, D), lambda i: (i, 0))
            in_specs += [spec, spec]
            args += [sc, sh]
        else:
            assert rows_per_mod % tm == 0
            rpt = rows_per_mod // tm
            spec = pl.BlockSpec((1, 1, D), lambda i: (i // rpt, 0, 0))
            in_specs += [spec, spec]
            args += [sc.reshape(-1, 1, D), sh.reshape(-1, 1, D)]
    out_shape = [jax.ShapeDtypeStruct((M, D), out_dtype)]
    out_specs = [pl.BlockSpec((tm, D), lambda i: (i, 0))]
    n_experts = 0
    if router is not None:
        w_r, b_r = router
        n_experts = w_r.shape[1]
        w_pad = jnp.zeros((D, LANE), F32).at[:, :n_experts].set(w_r)
        b_pad = jnp.zeros((1, LANE), F32).at[0, :n_experts].set(b_r)
        in_specs += [pl.BlockSpec((D, LANE), lambda i: (0, 0)), pl.BlockSpec((1, LANE), lambda i: (0, 0))]
        args += [w_pad, b_pad]
        out_shape += [jax.ShapeDtypeStruct((M, LANE), jnp.int32), jax.ShapeDtypeStruct((M, LANE), F32)]
        out_specs += [pl.BlockSpec((tm, LANE), lambda i: (i, 0))] * 2
    kern = functools.partial(_norm_kernel, modulated=modulated, router=router is not None,
                             n_experts=n_experts)
    res = pl.pallas_call(kern, grid=(M // tm,), in_specs=in_specs, out_specs=out_specs,
                         out_shape=out_shape, compiler_params=_cp("parallel"), name=name)(*args)
    return res if router is not None else res[0]


def _two_head_q(q):
    lane = lax.broadcasted_iota(jnp.int32, q.shape, 1)
    q0 = jnp.where(lane < 64, q, 0.0).astype(BF16)
    q1 = jnp.where(lane >= 64, q, 0.0).astype(BF16)
    return jnp.concatenate([q0, q1], axis=0)


def _sb_prompt_kernel(q_ref, k_ref, v_ref, g_ref, o_ref, acc_ref, carry_ref, *, t, n_pairs):
    qi = pl.program_id(1)
    r = lax.broadcasted_iota(jnp.int32, (t, t), 0)
    c = lax.broadcasted_iota(jnp.int32, (t, t), 1)
    upper = jnp.where(r > c, 1.0, 0.0).astype(BF16)
    upper2 = jnp.concatenate([upper, upper], axis=0)
    strict = jnp.concatenate([c < r, c < r], axis=0)
    n_tiles = t // LANE

    def lanes(x):
        return jnp.concatenate([x] * n_tiles, axis=1) if n_tiles > 1 else x
    qqs = [_two_head_q(q_ref[0, :, hp * LANE:(hp + 1) * LANE] * (1.0 / math.sqrt(64.0)))
           for hp in range(n_pairs)]

    def step(j, diagonal):
        start = pl.multiple_of(j * t, t)
        for hp in range(n_pairs):
            cols = slice(hp * LANE, (hp + 1) * LANE)
            k = k_ref[0, pl.ds(start, t), cols].astype(BF16)
            v = v_ref[0, pl.ds(start, t), cols].astype(BF16)
            z = lax.dot_general(qqs[hp], k, NT_DIMS, preferred_element_type=F32)
            lk = jnp.minimum(-z, 0.0) - jnp.log(1.0 + jnp.exp(-jnp.abs(z)))
            if diagonal:
                lk = jnp.where(strict, lk, 0.0)
            hi = lk.astype(BF16)
            lo = (lk - hi.astype(F32)).astype(BF16)
            la = jnp.dot(jnp.concatenate([hi, lo], axis=1), upper2, preferred_element_type=F32)
            total = jnp.broadcast_to(la[:, 0:1] + lk[:, 0:1], (2 * t, LANE))
            if diagonal:
                w = jnp.where(strict, jnp.exp(z + lk + la), 0.0)
                carry_ref[hp] = total
                acc_ref[hp] = jnp.dot(w.astype(BF16), v, preferred_element_type=F32)
            else:
                carry = carry_ref[hp]
                w = jnp.exp(z + lk + la + lanes(carry))
                carry_ref[hp] = carry + total
                acc_ref[hp] += jnp.dot(w.astype(BF16), v, preferred_element_type=F32)

    step(qi, True)

    def any_row_alive():
        top = carry_ref[0]
        for hp in range(1, n_pairs):
            top = jnp.maximum(top, carry_ref[hp])
        return (jnp.max(top) > F32_EXP_ZERO).astype(jnp.int32)

    def body(jj, alive):
        @pl.when(alive > 0)
        def _():
            step(qi - 1 - jj, False)
        return any_row_alive()

    lax.fori_loop(0, qi, body, any_row_alive())
    lane = lax.broadcasted_iota(jnp.int32, (t, LANE), 1)
    mean64 = _group_mean_matrix(LANE, 64)
    for hp in range(n_pairs):
        acc = acc_ref[hp]
        o = jnp.where(lane < 64, acc[:t], acc[t:])
        ms = jnp.dot((o * o).astype(BF16), mean64.astype(BF16), preferred_element_type=F32)
        cols = slice(hp * LANE, (hp + 1) * LANE)
        o_ref[:, cols] = (o * lax.rsqrt(ms + EPS) * g_ref[:, cols]).astype(o_ref.dtype)


def sb_attention_prompt(z3, gain, B, T, *, t=256):
    nq = T // t
    width = z3.shape[2]
    n_pairs = width // LANE
    kern = functools.partial(_sb_prompt_kernel, t=t, n_pairs=n_pairs)
    return pl.pallas_call(
        kern, grid=(B, nq),
        in_specs=[pl.BlockSpec((1, t, width), lambda b, i: (0, b * nq + i, 0)),
                  pl.BlockSpec((1, T, width), lambda b, i: (1, b, 0)),
                  pl.BlockSpec((1, T, width), lambda b, i: (2, b, 0)),
                  pl.BlockSpec((1, width), lambda b, i: (0, 0))],
        out_specs=pl.BlockSpec((t, width), lambda b, i: (b * nq + i, 0)),
        out_shape=jax.ShapeDtypeStruct((B * T, width), BF16),
        scratch_shapes=[pltpu.VMEM((n_pairs, 2 * t, LANE), F32)] * 2,
        compiler_params=_cp("parallel", "arbitrary"), name="sb_prompt")(
            z3, z3, z3, gain.reshape(1, -1))


def _diff_lambda(lv_ref):
    lv = lv_ref[...]
    s1 = jnp.sum(lv[0:1] * lv[1:2], axis=-1, keepdims=True)
    s2 = jnp.sum(lv[2:3] * lv[3:4], axis=-1, keepdims=True)
    return jnp.exp(s1) - jnp.exp(s2)


def _diff_prompt_kernel(q_ref, k_ref, v_ref, g_ref, lv_ref, o_ref, acc_ref, m_ref, *, t,
                        n_heads, lam_init):
    qi = pl.program_id(1)
    assert t <= 256
    r = lax.broadcasted_iota(jnp.int32, (2 * t, t), 0)
    r = jnp.where(r >= t, r - t, r)
    c = lax.broadcasted_iota(jnp.int32, (2 * t, t), 1)
    causal = c <= r
    kc = lax.broadcasted_iota(jnp.int32, (t, LANE), 0)
    kl = lax.broadcasted_iota(jnp.int32, (t, LANE), 1)
    k_pos = jnp.where(kl == 0, kc // 16, jnp.where(kl == 1, kc % 16, 0)).astype(BF16)
    ql = lax.broadcasted_iota(jnp.int32, (2 * t, LANE), 1)
    neg = jnp.float32(-jnp.inf)
    slopes = [2.0 ** (-8.0 * (h + 1) / n_heads) for h in range(n_heads)]
    q_augs = []
    for h in range(n_heads):
        qq = _two_head_q(q_ref[0, :, h * LANE:(h + 1) * LANE] * (1.0 / math.sqrt(64.0)))
        q_pos = jnp.where(ql == 0, 16.0 * slopes[h], jnp.where(ql == 1, slopes[h], 0.0)).astype(BF16)
        q_augs.append(jnp.concatenate([qq, q_pos], axis=1))

    ones = jnp.ones((t, LANE), BF16)
    n_tiles = t // LANE

    def lanes(x):
        return jnp.concatenate([x] * n_tiles, axis=1) if n_tiles > 1 else x

    def row_max(s):
        part = s[:, 0:LANE]
        for i in range(1, n_tiles):
            part = jnp.maximum(part, s[:, i * LANE:(i + 1) * LANE])
        return jnp.broadcast_to(jnp.max(part, axis=-1, keepdims=True), part.shape)

    def step(j, diagonal):
        start = pl.multiple_of(j * t, t)
        for h in range(n_heads):
            cols = slice(h * LANE, (h + 1) * LANE)
            k = jnp.concatenate([k_ref[0, pl.ds(start, t), cols].astype(BF16), k_pos], axis=1)
            v = jnp.concatenate([v_ref[0, pl.ds(start, t), cols].astype(BF16), ones], axis=1)
            s = lax.dot_general(q_augs[h], k, NT_DIMS, preferred_element_type=F32)
            off = slopes[h] * (j * t).astype(F32)
            if diagonal:
                s = jnp.where(causal, s, neg)
                m_loc = row_max(s)
                p = jnp.exp(s - lanes(m_loc))
                m_ref[h] = m_loc + off
                acc_ref[h] = jnp.dot(p.astype(BF16), v, preferred_element_type=F32)
            else:
                m_old = m_ref[h]
                m_new = jnp.maximum(m_old, row_max(s) + off)
                alpha = jnp.exp(m_old - m_new)
                p = jnp.exp(s - lanes(m_new - off))
                m_ref[h] = m_new
                acc_ref[h] = (jnp.concatenate([alpha, alpha], axis=1) * acc_ref[h]
                              + jnp.dot(p.astype(BF16), v, preferred_element_type=F32))

    step(qi, True)

    def body(jj, _):
        step(qi - 1 - jj, False)
        return 0

    lax.fori_loop(0, qi, body, 0)
    lam = _diff_lambda(lv_ref) + lam_init
    for h in range(n_heads):
        cols = slice(h * LANE, (h + 1) * LANE)
        acc = acc_ref[h]
        on = acc[:, :LANE] / acc[:, LANE:]
        o = on[:t] - lam * on[t:]
        ms = jnp.mean(o * o, axis=-1, keepdims=True)
        o_ref[:, cols] = (o * lax.rsqrt(ms + EPS) * g_ref[:, cols] * (1.0 - lam_init)).astype(o_ref.dtype)


def diff_attention_prompt(z3, gain, lam_vec, lam_init, B, T, *, t=256):
    nq = T // t
    width = z3.shape[2]
    n_heads = width // LANE
    kern = functools.partial(_diff_prompt_kernel, t=t, n_heads=n_heads, lam_init=lam_init)
    return pl.pallas_call(
        kern, grid=(B, nq),
        in_specs=[pl.BlockSpec((1, t, width), lambda b, i: (3, b * nq + i, 0)),
                  pl.BlockSpec((1, T, width), lambda b, i: (4, b, 0)),
                  pl.BlockSpec((1, T, width), lambda b, i: (5, b, 0)),
                  pl.BlockSpec((1, width), lambda b, i: (0, 0)),
                  pl.BlockSpec(lam_vec.shape, lambda b, i: (0, 0))],
        out_specs=pl.BlockSpec((t, width), lambda b, i: (b * nq + i, 0)),
        out_shape=jax.ShapeDtypeStruct((B * T, width), BF16),
        scratch_shapes=[pltpu.VMEM((n_heads, 2 * t, 2 * LANE), F32),
                        pltpu.VMEM((n_heads, 2 * t, LANE), F32)],
        compiler_params=_cp("parallel", "arbitrary"), name="diff_prompt")(
            z3, z3, z3, gain.reshape(1, -1), lam_vec)


def _gelu_tanh(x):
    return 0.5 * x * (1.0 + jnp.tanh(math.sqrt(2.0 / math.pi) * (x + 0.044715 * (x * x * x))))


MXU_WIDTH = 256


def _blockdiag_dot(x, w, precise):
    width = x.shape[-1]
    if precise or width % MXU_WIDTH:
        return jnp.dot(x, w, precision=HI, preferred_element_type=F32)
    parts = [jnp.dot(x[:, s:s + MXU_WIDTH].astype(BF16), w[s:s + MXU_WIDTH, s:s + MXU_WIDTH].astype(BF16),
                     preferred_element_type=F32) for s in range(0, width, MXU_WIDTH)]
    return jnp.concatenate(parts, axis=1)


def _lru_gates(xc, wr_ref, br_ref, wi_ref, bi_ref, lam_ref, precise):
    r = _sigmoid(_blockdiag_dot(xc, wr_ref[...], precise) + br_ref[...])
    i = _sigmoid(_blockdiag_dot(xc, wi_ref[...], precise) + bi_ref[...])
    log_a = -LRU_C * r * _softplus(-lam_ref[...])
    a = jnp.exp(log_a)
    u = jnp.sqrt(1.0 - jnp.exp(2.0 * log_a)) * (i * xc)
    return a, u


def _lru_finish(h, gate, g_ref, width, precise):
    y = h * _gelu_tanh(gate)
    ms = _blockdiag_dot(y * y, _group_mean_matrix(width, 64), precise)
    return y * lax.rsqrt(ms + EPS) * g_ref[...]


def _lru_prompt_kernel(x_ref, gate_ref, cw_ref, cb_ref, wr_ref, br_ref, wi_ref, bi_ref, lam_ref,
                       g_ref, y_ref, conv_ref, h_ref, xs_ref, hprev_ref, *, tc, width):
    ti = pl.program_id(1)

    @pl.when(ti == 0)
    def _():
        xs_ref[0:8, :] = jnp.zeros((8, width), F32)
        hprev_ref[...] = jnp.zeros_like(hprev_ref)

    @pl.when(ti > 0)
    def _():
        xs_ref[0:8, :] = xs_ref[tc:tc + 8, :]

    x = x_ref[0]
    xs_ref[8:8 + tc, :] = x
    xc = cb_ref[...] + x * cw_ref[3:4, :]
    for j in range(3):
        xc = xc + xs_ref[5 + j:5 + j + tc, :] * cw_ref[j:j + 1, :]
    a, u = _lru_gates(xc, wr_ref, br_ref, wi_ref, bi_ref, lam_ref, precise=False)
    row = lax.broadcasted_iota(jnp.int32, (tc, width), 0)
    d = 1
    while d < tc:
        valid = row >= d
        a_s = pltpu.roll(a, d, 0)
        u_s = pltpu.roll(u, d, 0)
        u = jnp.where(valid, a * u_s + u, u)
        a = jnp.where(valid, a * a_s, a)
        d *= 2
    h = u + a * hprev_ref[0:1, :]
    hprev_ref[...] = jnp.broadcast_to(h[tc - 1:tc, :], hprev_ref.shape)
    y_ref[...] = _lru_finish(h, gate_ref[0], g_ref, width, precise=False).astype(y_ref.dtype)

    @pl.when(ti == pl.num_programs(1) - 1)
    def _():
        conv_ref[0] = xs_ref[tc:tc + 8, :]
        h_ref[0] = h[tc - 8:tc, :]


def _block_diag(w):
    n, k, _ = w.shape
    eye = jnp.eye(n, dtype=w.dtype)
    return jnp.einsum('nkj,nm->nkmj', w, eye).reshape(n * k, n * k)


def lru_prompt(z3, p, l, B, T, *, tc=256):
    width = z3.shape[2]
    nt = T // tc
    row = lambda a: a.reshape(1, width)
    const = lambda shape: pl.BlockSpec(shape, lambda b, i: (0,) * len(shape))
    kern = functools.partial(_lru_prompt_kernel, tc=tc, width=width)
    y, conv, h = pl.pallas_call(
        kern, grid=(B, nt),
        in_specs=[pl.BlockSpec((1, tc, width), lambda b, i: (6, b * nt + i, 0)),
                  pl.BlockSpec((1, tc, width), lambda b, i: (7, b * nt + i, 0)),
                  const((4, width)), const((1, width)), const((width, width)), const((1, width)),
                  const((width, width)), const((1, width)), const((1, width)), const((1, width))],
        out_specs=[pl.BlockSpec((tc, width), lambda b, i: (b * nt + i, 0)),
                   pl.BlockSpec((1, 8, width), lambda b, i: (b, 0, 0)),
                   pl.BlockSpec((1, 8, width), lambda b, i: (b, 0, 0))],
        out_shape=[jax.ShapeDtypeStruct((B * T, width), BF16),
                   jax.ShapeDtypeStruct((B, 8, width), F32),
                   jax.ShapeDtypeStruct((B, 8, width), F32)],
        scratch_shapes=[pltpu.VMEM((tc + 8, width), F32), pltpu.VMEM((8, width), F32)],
        compiler_params=_cp("parallel", "arbitrary"), name="lru_prompt")(
            z3, z3, p['lru_conv_w'][l], row(p['lru_conv_b'][l]), _block_diag(p['lru_w_r'][l]),
            row(p['lru_b_r'][l]), _block_diag(p['lru_w_i'][l]), row(p['lru_b_i'][l]),
            row(p['lru_lam'][l]), row(p['g_lru_out'][l]))
    return y, conv[:, 5:8, :], h[:, 7, :]


def _lru_step_kernel(x_ref, gate_ref, conv_ref, h0_ref, cw_ref, cb_ref, wr_ref, br_ref, wi_ref,
                     bi_ref, lam_ref, g_ref, y_ref, h_ref, *, width):
    xc = cb_ref[...] + x_ref[...] * cw_ref[3:4, :]
    for j in range(3):
        xc = xc + conv_ref[j] * cw_ref[j:j + 1, :]
    a, u = _lru_gates(xc, wr_ref, br_ref, wi_ref, bi_ref, lam_ref, precise=True)
    h = a * h0_ref[...] + u
    h_ref[...] = h
    y_ref[...] = _lru_finish(h, gate_ref[...], g_ref, width, precise=True)


def lru_step(x, gate, conv0, h0, p, l):
    G, width = x.shape
    row = lambda a: a.reshape(1, width)
    kern = functools.partial(_lru_step_kernel, width=width)
    y, h = pl.pallas_call(
        kern, out_shape=[jax.ShapeDtypeStruct((G, width), F32)] * 2, name="lru_step")(
            x, gate, conv0.swapaxes(0, 1), h0, p['lru_conv_w'][l], row(p['lru_conv_b'][l]),
            _block_diag(p['lru_w_r'][l]), row(p['lru_b_r'][l]), _block_diag(p['lru_w_i'][l]),
            row(p['lru_b_i'][l]), row(p['lru_lam'][l]), row(p['g_lru_out'][l]))
    conv_new = jnp.concatenate([conv0[:, 1:], x[:, None, :]], axis=1)
    return y, conv_new, h


def _log_sigmoid(x):
    return -_softplus(-x)


def _mlstm_prompt_kernel(q_ref, k_ref, v_ref, og_ref, gc_ref, gr_ref, g_ref, y_ref, c_out, n_out,
                         m_out, c_ref, n_ref, m_ref, *, L, nc, d, n_hh):
    c_ref[...] = jnp.zeros_like(c_ref)
    n_ref[...] = jnp.zeros_like(n_ref)
    m_ref[...] = jnp.zeros_like(m_ref)
    r = lax.broadcasted_iota(jnp.int32, (L, L), 0)
    cc = lax.broadcasted_iota(jnp.int32, (L, L), 1)
    causal = cc <= r
    neg = jnp.float32(-jnp.inf)

    def chunk_head(start, hh):
        cols = slice(hh * d, (hh + 1) * d)
        q = q_ref[0, pl.ds(start, L), cols]
        ks = k_ref[0, pl.ds(start, L), cols] * (1.0 / math.sqrt(d))
        v = v_ref[0, pl.ds(start, L), cols]
        gcol = gc_ref[0, hh, pl.ds(start, L), :]
        grow = gr_ref[0, hh, :, pl.ds(start, L)]
        i_col, i_row = gcol[:, 0:1], grow[0:1, :]
        lf_col, lf_row = _log_sigmoid(gcol[:, 1:2]), _log_sigmoid(grow[1:2, :])
        b_col = jnp.sum(jnp.where(causal, lf_row, 0.0), axis=-1, keepdims=True)
        b_row = jnp.sum(jnp.where(r <= cc, lf_col, 0.0), axis=0, keepdims=True)
        m_prev = m_ref[hh, 0:1, 0:1]
        dmat = jnp.where(causal, b_col - b_row + i_row, neg)
        m_t = jnp.maximum(b_col + m_prev, jnp.max(dmat, axis=-1, keepdims=True))
        w = jnp.exp(dmat - m_t)
        inter = jnp.exp(b_col + m_prev - m_t)
        qb = q.astype(BF16)
        s = lax.dot_general(qb, ks.astype(BF16), NT_DIMS, preferred_element_type=F32)
        sw = w * s
        c = c_ref[hh]
        n = n_ref[hh, 0:1, :]
        num = (jnp.dot(sw.astype(BF16), v.astype(BF16), preferred_element_type=F32)
               + inter * lax.dot_general(qb, c.astype(BF16), NT_DIMS, preferred_element_type=F32))
        den = jnp.sum(sw, axis=-1, keepdims=True) + inter * jnp.sum(q * n, axis=-1, keepdims=True)
        h = num / jnp.maximum(jnp.abs(den), jnp.exp(-m_t))
        m_new = m_t[L - 1:L, :]
        decay = inter[L - 1:L, :]
        w_last = jnp.exp(b_col[L - 1:L, :] - b_col + i_col - m_new)
        c_ref[hh] = decay * c + _dot3((v * w_last).T, ks)
        n_ref[hh] = jnp.broadcast_to(decay * n + jnp.sum(ks * w_last, axis=0, keepdims=True),
                                     n_ref.shape[1:])
        m_ref[hh] = jnp.broadcast_to(m_new, m_ref.shape[1:])
        y = h * _sigmoid(og_ref[0, pl.ds(start, L), cols])
        ms = jnp.mean(y * y, axis=-1, keepdims=True)
        y_ref[pl.ds(start, L), cols] = (y * lax.rsqrt(ms + EPS) * g_ref[:, cols]).astype(y_ref.dtype)

    def chunk(ci, _):
        start = pl.multiple_of(ci * L, L)
        for hh in range(n_hh):
            chunk_head(start, hh)
        return 0

    lax.fori_loop(0, nc, chunk, 0)
    c_out[0] = c_ref[...]
    n_out[0] = n_ref[...]
    m_out[0] = m_ref[...]


def mlstm_prompt(z3, gates, gain, B, T, *, L=128):
    H = z3.shape[2] // LANE
    d = LANE
    g4 = gates.reshape(B, T, 2, H)
    gcol = g4.transpose(0, 3, 1, 2)
    grow = g4.transpose(0, 3, 2, 1)
    n_hh = 2 if H % 2 == 0 else 1
    kern = functools.partial(_mlstm_prompt_kernel, L=L, nc=T // L, d=d, n_hh=n_hh)
    col = lambda grp: pl.BlockSpec((1, T, n_hh * d), lambda b, h: (grp, b, h))
    y, c, n, m = pl.pallas_call(
        kern, grid=(B, H // n_hh),
        in_specs=[col(8), col(9), col(10), col(11),
                  pl.BlockSpec((1, n_hh, T, 2), lambda b, h: (b, h, 0, 0)),
                  pl.BlockSpec((1, n_hh, 2, T), lambda b, h: (b, h, 0, 0)),
                  pl.BlockSpec((1, n_hh * d), lambda b, h: (0, h))],
        out_specs=[pl.BlockSpec((T, n_hh * d), lambda b, h: (b, h)),
                   pl.BlockSpec((1, n_hh, d, d), lambda b, h: (b, h, 0, 0)),
                   pl.BlockSpec((1, n_hh, 8, d), lambda b, h: (b, h, 0, 0)),
                   pl.BlockSpec((1, n_hh, 8, LANE), lambda b, h: (b, h, 0, 0))],
        out_shape=[jax.ShapeDtypeStruct((B * T, H * d), BF16),
                   jax.ShapeDtypeStruct((B, H, d, d), F32),
                   jax.ShapeDtypeStruct((B, H, 8, d), F32),
                   jax.ShapeDtypeStruct((B, H, 8, LANE), F32)],
        scratch_shapes=[pltpu.VMEM((n_hh, d, d), F32), pltpu.VMEM((n_hh, 8, d), F32),
                        pltpu.VMEM((n_hh, 8, LANE), F32)],
        compiler_params=_cp("parallel", "parallel"), name="mlstm_prompt")(
            z3, z3, z3, z3, gcol, grow, gain.reshape(1, -1))
    return y, c, n[:, :, 0, :], m[:, :, 0, 0]


def _mlstm_step_kernel(q_ref, k_ref, v_ref, og_ref, i_ref, f_ref, c_ref, n_ref, m_ref, g_ref,
                       y_ref, c_out, n_out, m_out, *, d):
    q, v = q_ref[0], v_ref[0]
    k = k_ref[0] * (1.0 / math.sqrt(d))
    i_pre, lf = i_ref[0], _log_sigmoid(f_ref[0])
    c, n, m = c_ref[0], n_ref[0], m_ref[0]
    m_t = jnp.maximum(lf + m, i_pre)
    w = jnp.exp(i_pre - m_t)
    inter = jnp.exp(lf + m - m_t)
    sw = w * jnp.sum(q * k, axis=-1, keepdims=True)
    eye = (lax.broadcasted_iota(jnp.int32, (1, d, d), 1) == lax.broadcasted_iota(jnp.int32, (1, d, d), 2))
    cq_col = jnp.sum(c * q, axis=-1, keepdims=True)
    cq = jnp.sum(jnp.where(eye, cq_col, 0.0), axis=1, keepdims=True)
    num = sw * v + inter * cq
    den = sw + inter * jnp.sum(n * q, axis=-1, keepdims=True)
    h = num / jnp.maximum(jnp.abs(den), jnp.exp(-m_t))
    v_col = jnp.sum(jnp.where(eye, v, 0.0), axis=-1, keepdims=True)
    c_out[0] = inter * c + (w * v_col) * k
    n_out[0] = inter * n + w * k
    m_out[0] = m_t
    y = h * _sigmoid(og_ref[0])
    ms = jnp.mean(y * y, axis=-1, keepdims=True)
    y_ref[0] = y * lax.rsqrt(ms + EPS) * g_ref[...]


def mlstm_step(q, k, v, og, i_pre, f_pre, c0, n0, m0, gain):
    G, H, d = n0.shape
    vec = lambda a: a.reshape(G, H, 1, d)
    sca = lambda a: a.reshape(G, H, 1, 1)
    vspec = pl.BlockSpec((1, H, 1, d), lambda b: (b, 0, 0, 0))
    sspec = pl.BlockSpec((1, H, 1, 1), lambda b: (b, 0, 0, 0))
    cspec = pl.BlockSpec((1, H, d, d), lambda b: (b, 0, 0, 0))
    kern = functools.partial(_mlstm_step_kernel, d=d)
    y, c, n, m = pl.pallas_call(
        kern, grid=(G,),
        in_specs=[vspec, vspec, vspec, vspec, sspec, sspec, cspec, vspec, sspec,
                  pl.BlockSpec((H, 1, d), lambda b: (0, 0, 0))],
        out_specs=[vspec, cspec, vspec, sspec],
        out_shape=[jax.ShapeDtypeStruct((G, H, 1, d), F32), jax.ShapeDtypeStruct((G, H, d, d), F32),
                   jax.ShapeDtypeStruct((G, H, 1, d), F32), jax.ShapeDtypeStruct((G, H, 1, 1), F32)],
        compiler_params=_cp("parallel"), name="mlstm_step")(
            vec(q), vec(k), vec(v), vec(og), sca(i_pre), sca(f_pre), c0, vec(n0), sca(m0),
            gain.reshape(H, 1, d))
    return y.reshape(G, H * d), c, n.reshape(G, H, d), m.reshape(G, H)


def _page_specs(P, n_pages, block):
    zeros = (0,) * (len(block) - 1)

    def spec(p):
        return pl.BlockSpec(block, lambda b, s, rows: (rows[b * n_pages + s * P + p],) + zeros)
    return [spec(p) for p in range(P)]


def _decode_scores_kernel(rows_ref, q_ref, *rest, P, page):
    pages, o_ref = rest[:P], rest[P]
    n_grp, d = q_ref.shape[1], q_ref.shape[2]
    q = jnp.broadcast_to(q_ref[0] * (1.0 / math.sqrt(d)), (n_grp, d, page))
    for p in range(P):
        o_ref[0, :, :, p * page:(p + 1) * page] = jnp.sum(pages[p][0] * q, axis=1, keepdims=True)


def decode_scores(q, pool_t, rows, n_pages, *, P=64):
    G = q.shape[0]
    _, n_grp, d, page = pool_t.shape
    P = _pick_tile(n_pages, P)
    kern = functools.partial(_decode_scores_kernel, P=P, page=page)
    grid_spec = pltpu.PrefetchScalarGridSpec(
        num_scalar_prefetch=1, grid=(G, n_pages // P),
        in_specs=[pl.BlockSpec((1, n_grp, d, 1), lambda b, s, rows: (b, 0, 0, 0))]
        + _page_specs(P, n_pages, (1, n_grp, d, page)),
        out_specs=pl.BlockSpec((1, n_grp, 1, P * page), lambda b, s, rows: (b, 0, 0, s)))
    return pl.pallas_call(
        kern, grid_spec=grid_spec,
        out_shape=jax.ShapeDtypeStruct((G, n_grp, 1, n_pages * page), F32),
        compiler_params=_cp("parallel", "arbitrary"), name="decode_scores")(
            rows, q.reshape(G, n_grp, d, 1), *([pool_t] * P))


def _sb_weights_kernel(z_ref, w_ref, *, n_tiles):
    bb = z_ref.shape[0]
    r = lax.broadcasted_iota(jnp.int32, (LANE, LANE), 0)
    c = lax.broadcasted_iota(jnp.int32, (LANE, LANE), 1)
    upper = jnp.where(r > c, 1.0, 0.0).astype(F32)
    carry = jnp.zeros((bb * 8, 1), F32)
    for t in reversed(range(n_tiles)):
        z = z_ref[:, :, t * LANE:(t + 1) * LANE].reshape(bb * 8, LANE)
        lk = -_softplus(z)
        la = jnp.dot(lk, upper, precision=HI, preferred_element_type=F32) + carry
        w_ref[:, :, t * LANE:(t + 1) * LANE] = jnp.exp(z + lk + la).reshape(bb, 8, LANE)
        carry = carry + jnp.sum(lk, axis=-1, keepdims=True)


def sb_weights(z, *, bb=8):
    G, _, Tp = z.shape
    bb = _pick_tile(G, bb)
    kern = functools.partial(_sb_weights_kernel, n_tiles=Tp // LANE)
    spec = pl.BlockSpec((bb, 8, Tp), lambda i: (i, 0, 0))
    return pl.pallas_call(kern, grid=(G // bb,), in_specs=[spec], out_specs=spec,
                          out_shape=jax.ShapeDtypeStruct(z.shape, F32),
                          compiler_params=_cp("parallel"), name="sb_weights")(z)


def _diff_weights_kernel(z_ref, q_ref, k_ref, v_ref, lv_ref, w_ref, new_ref, *, lam_init):
    bb, _, Tp = z_ref.shape
    width = q_ref.shape[-1]
    lam = (_diff_lambda(lv_ref) + lam_init).reshape(1, 1, 1)
    prod = q_ref[...] * k_ref[...] * (1.0 / math.sqrt(64.0))
    grp = lax.broadcasted_iota(jnp.int32, (1, 8, width), 2) // 64
    r8 = lax.broadcasted_iota(jnp.int32, (1, 8, width), 1)
    mine = grp == (r8 % 4) * 2 + r8 // 4
    s_new = jnp.sum(jnp.where(mine, prod, 0.0), axis=-1, keepdims=True)
    head = lax.broadcasted_iota(jnp.int32, (1, 8, 1), 1) % 4
    slope = jnp.exp((-2.0 * math.log(2.0)) * (head + 1).astype(F32))
    kpos = lax.broadcasted_iota(jnp.int32, (1, 1, Tp), 2)
    s = z_ref[...] - slope * (Tp - kpos).astype(F32)
    m = jnp.maximum(jnp.max(s, axis=-1, keepdims=True), s_new)
    p = jnp.exp(s - m)
    p_new = jnp.exp(s_new - m)
    den = jnp.sum(p, axis=-1, keepdims=True) + p_new
    p = p / den
    p_new = p_new / den
    w = p[:, 0:4, :] - lam * p[:, 4:8, :]
    w8 = jnp.concatenate([w, jnp.zeros_like(w)], axis=1).reshape(bb * 8, Tp)
    n_heads = 4
    spread = (lax.broadcasted_iota(jnp.int32, (LANE, n_heads * LANE), 1) // n_heads
              == lax.broadcasted_iota(jnp.int32, (LANE, n_heads * LANE), 0)).astype(F32)
    keep = (lax.broadcasted_iota(jnp.int32, (bb * 8, n_heads * LANE), 1) % n_heads
            == lax.broadcasted_iota(jnp.int32, (bb * 8, n_heads * LANE), 0) % 8)
    for t in range(Tp // LANE):
        wide = jnp.dot(w8[:, t * LANE:(t + 1) * LANE], spread, precision=HI,
                       preferred_element_type=F32)
        w_ref[:, :, t * n_heads * LANE:(t + 1) * n_heads * LANE] = jnp.where(keep, wide, 0.0).reshape(
            bb, 8, n_heads * LANE)
    w_new = p_new[:, 0:4, :] - lam * p_new[:, 4:8, :]
    new_ref[...] = w_new * v_ref[...]


def diff_weights(z, q, k_new, v_new, lam_vec, lam_init, *, bb=8):
    G, _, Tp = z.shape
    width = q.shape[-1]
    bb = _pick_tile(G, bb)
    kern = functools.partial(_diff_weights_kernel, lam_init=lam_init)
    zspec = pl.BlockSpec((bb, 8, Tp), lambda i: (i, 0, 0))
    qspec = pl.BlockSpec((bb, 1, width), lambda i: (i, 0, 0))
    vspec = pl.BlockSpec((bb, 4, LANE), lambda i: (i, 0, 0))
    r3 = lambda a: a.reshape(G, 1, width)
    return pl.pallas_call(
        kern, grid=(G // bb,),
        in_specs=[zspec, qspec, qspec, vspec, pl.BlockSpec(lam_vec.shape, lambda i: (0, 0))],
        out_specs=[pl.BlockSpec((bb, 8, 4 * Tp), lambda i: (i, 0, 0)), vspec],
        out_shape=[jax.ShapeDtypeStruct((G, 8, 4 * Tp), F32), jax.ShapeDtypeStruct((G, 4, LANE), F32)],
        compiler_params=_cp("parallel"), name="diff_weights")(
            z, r3(q), r3(k_new), v_new.reshape(G, 4, LANE), lam_vec)


def _decode_pv_t_kernel(rows_ref, w_ref, *rest, P, page):
    pages, o_ref, acc_ref = rest[:P], rest[P], rest[P + 1]
    s = pl.program_id(1)

    @pl.when(s == 0)
    def _():
        acc_ref[...] = jnp.zeros_like(acc_ref)

    acc = acc_ref[...]
    for p in range(P):
        acc = acc + pages[p][0] * w_ref[0, :, :, p * page:(p + 1) * page]
    acc_ref[...] = acc

    @pl.when(s == pl.num_programs(1) - 1)
    def _():
        d = acc.shape[1]
        col = jnp.sum(acc, axis=-1, keepdims=True)
        eye = (lax.broadcasted_iota(jnp.int32, (1, d, d), 1)
               == lax.broadcasted_iota(jnp.int32, (1, d, d), 2))
        o_ref[0] = jnp.sum(jnp.where(eye, col, 0.0), axis=1, keepdims=True)


def decode_pv_t(w, pool_t, rows, n_pages, *, P=64):
    G = w.shape[0]
    _, n_grp, d, page = pool_t.shape
    P = _pick_tile(n_pages, P)
    kern = functools.partial(_decode_pv_t_kernel, P=P, page=page)
    grid_spec = pltpu.PrefetchScalarGridSpec(
        num_scalar_prefetch=1, grid=(G, n_pages // P),
        in_specs=[pl.BlockSpec((1, n_grp, 1, P * page), lambda b, s, rows: (b, 0, 0, s))]
        + _page_specs(P, n_pages, (1, n_grp, d, page)),
        out_specs=pl.BlockSpec((1, n_grp, 1, d), lambda b, s, rows: (b, 0, 0, 0)),
        scratch_shapes=[pltpu.VMEM((n_grp, d, page), F32)])
    out = pl.pallas_call(
        kern, grid_spec=grid_spec, out_shape=jax.ShapeDtypeStruct((G, n_grp, 1, d), F32),
        compiler_params=_cp("parallel", "arbitrary"), name="decode_pv_t")(rows, w, *([pool_t] * P))
    return out.reshape(G, n_grp * d)


def _decode_pv_rows_kernel(rows_ref, w_ref, init_ref, *rest, P, n_rows):
    pages, o_ref, acc_ref = rest[:P], rest[P], rest[P + 1]
    s = pl.program_id(1)

    @pl.when(s == 0)
    def _():
        acc_ref[...] = jnp.zeros_like(acc_ref)

    acc = acc_ref[...]
    for p in range(P):
        acc = acc + _dot3(w_ref[0, :, p * n_rows:(p + 1) * n_rows], pages[p][0])
    acc_ref[...] = acc

    @pl.when(s == pl.num_programs(1) - 1)
    def _():
        o_ref[0] = acc[0:4] + init_ref[0]


def decode_pv_rows(w, pool, rows, init, n_pages, *, P=64):
    G = w.shape[0]
    _, n_rows, width = pool.shape
    P = _pick_tile(n_pages, P)
    kern = functools.partial(_decode_pv_rows_kernel, P=P, n_rows=n_rows)
    grid_spec = pltpu.PrefetchScalarGridSpec(
        num_scalar_prefetch=1, grid=(G, n_pages // P),
        in_specs=[pl.BlockSpec((1, 8, P * n_rows), lambda b, s, rows: (b, 0, s)),
                  pl.BlockSpec((1, 4, width), lambda b, s, rows: (b, 0, 0))]
        + _page_specs(P, n_pages, (1, n_rows, width)),
        out_specs=pl.BlockSpec((1, 4, width), lambda b, s, rows: (b, 0, 0)),
        scratch_shapes=[pltpu.VMEM((8, width), F32)])
    out = pl.pallas_call(
        kern, grid_spec=grid_spec, out_shape=jax.ShapeDtypeStruct((G, 4, width), F32),
        compiler_params=_cp("parallel", "arbitrary"), name="decode_pv_rows")(
            rows, w, init, *([pool] * P))
    return out.reshape(G, 4 * width)


def _group_norm_kernel(x_ref, g_ref, o_ref, *, group, post_scale):
    x = x_ref[...]
    ms = jnp.dot(x * x, _group_mean_matrix(x.shape[-1], group), precision=HI,
                 preferred_element_type=F32)
    o_ref[...] = x * lax.rsqrt(ms + EPS) * g_ref[...] * post_scale


def _moe_up_kernel(te_ref, act_ref, first_ref, a_ref, wg_ref, wu_ref, o_ref, wg_bf, wu_bf):
    i = pl.program_id(1)

    @pl.when(first_ref[i] > 0)
    def _():
        wg_bf[...] = wg_ref[...].astype(BF16)
        wu_bf[...] = wu_ref[...].astype(BF16)

    @pl.when(act_ref[i] > 0)
    def _():
        a = a_ref[...].astype(BF16)
        g = jnp.dot(a, wg_bf[...], preferred_element_type=F32)
        u = jnp.dot(a, wu_bf[...], preferred_element_type=F32)
        o_ref[...] = (g * _sigmoid(g) * u).astype(o_ref.dtype)

    @pl.when(act_ref[i] == 0)
    def _():
        o_ref[...] = jnp.zeros_like(o_ref)


def _moe_down_kernel(te_ref, act_ref, first_ref, a_ref, wd_ref, o_ref, wd_bf):
    i = pl.program_id(1)

    @pl.when(first_ref[i] > 0)
    def _():
        wd_bf[...] = wd_ref[...].astype(BF16)

    @pl.when(act_ref[i] > 0)
    def _():
        o_ref[...] = jnp.dot(a_ref[...], wd_bf[...], preferred_element_type=F32)

    @pl.when(act_ref[i] == 0)
    def _():
        o_ref[...] = jnp.zeros_like(o_ref)


def moe_grouped(xg, tile_expert, tile_active, tile_first, wg, wu, wd, *, tg, tn_up=1408, tn_down=1024):
    n_rows, D = xg.shape
    F = wg.shape[-1]
    n_tiles = n_rows // tg
    tn_up, tn_down = _pick_tile(F, tn_up), _pick_tile(D, tn_down)
    once = pl.Buffered(1)
    up_spec = pltpu.PrefetchScalarGridSpec(
        num_scalar_prefetch=3, grid=(F // tn_up, n_tiles),
        in_specs=[pl.BlockSpec((tg, D), lambda j, i, te, ac, fi: (i, 0)),
                  pl.BlockSpec((None, D, tn_up), lambda j, i, te, ac, fi: (te[i], 0, j), pipeline_mode=once),
                  pl.BlockSpec((None, D, tn_up), lambda j, i, te, ac, fi: (te[i], 0, j), pipeline_mode=once)],
        out_specs=pl.BlockSpec((tg, tn_up), lambda j, i, te, ac, fi: (i, j)),
        scratch_shapes=[pltpu.VMEM((D, tn_up), BF16), pltpu.VMEM((D, tn_up), BF16)])
    act = pl.pallas_call(_moe_up_kernel, grid_spec=up_spec,
                         out_shape=jax.ShapeDtypeStruct((n_rows, F), BF16),
                         compiler_params=_cp("arbitrary", "arbitrary"), name="moe_up")(
                             tile_expert, tile_active, tile_first, xg, wg, wu)
    down_spec = pltpu.PrefetchScalarGridSpec(
        num_scalar_prefetch=3, grid=(D // tn_down, n_tiles),
        in_specs=[pl.BlockSpec((tg, F), lambda j, i, te, ac, fi: (i, 0)),
                  pl.BlockSpec((None, F, tn_down), lambda j, i, te, ac, fi: (te[i], 0, j))],
        out_specs=pl.BlockSpec((tg, tn_down), lambda j, i, te, ac, fi: (i, j)),
        scratch_shapes=[pltpu.VMEM((F, tn_down), BF16)])
    return pl.pallas_call(_moe_down_kernel, grid_spec=down_spec,
                          out_shape=jax.ShapeDtypeStruct((n_rows, D), F32),
                          compiler_params=_cp("arbitrary", "arbitrary"), name="moe_down")(
                              tile_expert, tile_active, tile_first, act, wd)


def _moe_dispatch(idx, n_experts, tg):
    M = idx.shape[0]
    flat_e = idx.reshape(-1)
    onehot = (flat_e[:, None] == jnp.arange(n_experts, dtype=jnp.int32)[None, :]).astype(jnp.int32)
    counts = jnp.sum(onehot, axis=0)
    rank = jnp.sum((jnp.cumsum(onehot, axis=0) - onehot) * onehot, axis=1)
    padded = (counts + tg - 1) // tg * tg
    ends = jnp.cumsum(padded)
    pos = (ends - padded)[flat_e] + rank
    n_rows = 2 * M + n_experts * tg
    row_token = jnp.zeros((n_rows,), jnp.int32).at[pos].set(jnp.arange(2 * M, dtype=jnp.int32) // 2)
    tile_start = jnp.arange(n_rows // tg, dtype=jnp.int32) * tg
    tile_expert = jnp.minimum(jnp.sum((tile_start[:, None] >= ends[None, :]).astype(jnp.int32), axis=1),
                              n_experts - 1)
    tile_expert = tile_expert.astype(jnp.int32)
    tile_active = (tile_start < ends[-1]).astype(jnp.int32)
    tile_first = jnp.concatenate([jnp.ones((1,), jnp.int32),
                                  (tile_expert[1:] != tile_expert[:-1]).astype(jnp.int32)])
    return pos.reshape(M, 2), row_token, tile_expert, tile_active, tile_first


def _moe_combine_kernel(x_ref, g_ref, y1_ref, y2_ref, w_ref, o_ref):
    w = w_ref[...]
    f = w[:, 0:1] * y1_ref[...] + w[:, 1:2] * y2_ref[...]
    o_ref[...] = x_ref[...] + g_ref[0] * f


def moe_combine(x, g, y1, y2, wts, rows_per_gate, *, tm=512):
    M, D = x.shape
    tm = _pick_tile(rows_per_gate, tm)
    rpt = rows_per_gate // tm
    row = pl.BlockSpec((tm, D), lambda i: (i, 0))
    return pl.pallas_call(
        _moe_combine_kernel, grid=(M // tm,),
        in_specs=[row, pl.BlockSpec((1, 1, D), lambda i: (i // rpt, 0, 0)), row, row,
                  pl.BlockSpec((tm, LANE), lambda i: (i, 0))],
        out_specs=row, out_shape=jax.ShapeDtypeStruct((M, D), F32),
        compiler_params=_cp("parallel"), name="moe_combine")(x, g.reshape(-1, 1, D), y1, y2, wts)


def _moe_dense_up_kernel(a_ref, wg_ref, wu_ref, o_ref, accg_ref, accu_ref, *, nk):
    k = pl.program_id(2)
    a = a_ref[...]
    pg = _dot3(a, wg_ref[...])
    pu = _dot3(a, wu_ref[...])

    @pl.when(k == 0)
    def _():
        accg_ref[...] = pg
        accu_ref[...] = pu

    @pl.when(k > 0)
    def _():
        accg_ref[...] += pg
        accu_ref[...] += pu

    @pl.when(k == nk - 1)
    def _():
        g = accg_ref[...]
        o_ref[...] = g * _sigmoid(g) * accu_ref[...]


def _moe_dense_down_kernel(a_ref, wd_ref, cw_ref, x_ref, g_ref, o_ref, acc_ref):
    e, k = pl.program_id(0), pl.program_id(1)
    part = cw_ref[...] * _dot3(a_ref[...], wd_ref[...])

    @pl.when((e == 0) & (k == 0))
    def _():
        acc_ref[...] = part

    @pl.when((e > 0) | (k > 0))
    def _():
        acc_ref[...] += part

    @pl.when((e == pl.num_programs(0) - 1) & (k == pl.num_programs(1) - 1))
    def _():
        o_ref[...] = x_ref[...] + g_ref[...] * acc_ref[...]


def moe_dense(h, x, gate, combine, wg, wu, wd, *, tn=1408, tk=1024, tkd=1408):
    M, D = h.shape
    E, _, F = wg.shape
    tn, tk, tkd = _pick_tile(F, tn), _pick_tile(D, tk), _pick_tile(F, tkd)
    nk = D // tk
    wspec = pl.BlockSpec((None, tk, tn), lambda e, j, k: (e, k, j))
    act = pl.pallas_call(
        functools.partial(_moe_dense_up_kernel, nk=nk), grid=(E, F // tn, nk),
        in_specs=[pl.BlockSpec((M, tk), lambda e, j, k: (0, k)), wspec, wspec],
        out_specs=pl.BlockSpec((None, M, tn), lambda e, j, k: (e, 0, j)),
        out_shape=jax.ShapeDtypeStruct((E, M, F), F32),
        scratch_shapes=[pltpu.VMEM((M, tn), F32), pltpu.VMEM((M, tn), F32)],
        compiler_params=_cp("parallel", "parallel", "arbitrary"), name="moe_dense_up")(h, wg, wu)
    full = pl.BlockSpec((M, D), lambda e, k: (0, 0))
    return pl.pallas_call(
        _moe_dense_down_kernel, grid=(E, F // tkd),
        in_specs=[pl.BlockSpec((None, M, tkd), lambda e, k: (e, 0, k)),
                  pl.BlockSpec((None, tkd, D), lambda e, k: (e, k, 0)),
                  pl.BlockSpec((None, M, 1), lambda e, k: (e, 0, 0)), full, full],
        out_specs=full, out_shape=jax.ShapeDtypeStruct((M, D), F32),
        scratch_shapes=[pltpu.VMEM((M, D), F32)],
        compiler_params=_cp("arbitrary", "arbitrary"), name="moe_dense_down")(
            act, wd, combine, x, gate)


def group_norm(x, gain, *, group, post_scale=1.0):
    kern = functools.partial(_group_norm_kernel, group=group, post_scale=post_scale)
    return pl.pallas_call(kern, out_shape=jax.ShapeDtypeStruct(x.shape, F32), name="group_norm")(
        x, gain.reshape(1, -1))


W_GROUP = 512
N_MAIN = 12 * W_GROUP
MOE_TILE = 256


def _lam_init(l):
    return 0.8 - 0.6 * math.exp(-0.3 * l)


def _gate_weight(w_in, l):
    wg = w_in.swapaxes(1, 2)[l, N_MAIN:, :].T
    return jnp.zeros((wg.shape[0], LANE), F32).at[:, :wg.shape[1]].set(wg)


def _prompt_trunk(x3, mods, p, wb):
    B, T, D = x3.shape
    M = B * T
    x = x3.reshape(M, D)
    news = []
    for l in range(len(mods)):
        sh1, sc1, g1, sh2, sc2, g2 = mods[l]
        h = norm_mod(x, p['g_norm1'][l], sc1, sh1, out_dtype=BF16, rows_per_mod=T, name="norm1_p")
        z3, sbk_t, sbv_t, dfk_t = w_in_prompt(h, wb['w_in'], l, B, T, n_groups=12, width=W_GROUP,
                                              kt_groups=(1, 2, 4))
        gates = matmul(h, _gate_weight(p['w_in'], l).astype(BF16), precise=False, name="gates_p")
        gates = gates[:, :8] + jnp.concatenate([p['ml_b_i'][l], p['ml_b_f'][l]])[None, :]
        y_sb = sb_attention_prompt(z3, p['g_sb_out'][l], B, T)
        y_df = diff_attention_prompt(z3, p['g_diff_out'][l], p['diff_lam'][l], _lam_init(l), B, T)
        y_lru, conv_new, h_new = lru_prompt(z3, p, l, B, T)
        y_ml, c_new, n_new, m_new = mlstm_prompt(z3, gates, p['g_ml_out'][l], B, T)
        ycat = jnp.concatenate([y_sb, y_df, y_lru, y_ml], axis=-1)
        x = matmul(ycat, wb['w_out'], precise=False, layer=l, resid=(x, g1), rows_per_gate=T,
                   name="w_out_p")
        j = l // 2
        if l % 2 == 0:
            h2 = norm_mod(x, p['g_norm2'][l], sc2, sh2, out_dtype=BF16, rows_per_mod=T, name="norm2_p")
            act = swiglu_up(h2, wb['ffn_w_gate'], wb['ffn_w_up'], precise=False, out_dtype=BF16,
                            layer=j, name="ffn_up_p")
            x = matmul(act, wb['ffn_w_down'], precise=False, layer=j, tk=2816, resid=(x, g2),
                       rows_per_gate=T, name="ffn_down_p")
        else:
            h2, idx, wts = norm_mod(x, p['g_norm2'][l], sc2, sh2, out_dtype=F32, rows_per_mod=T,
                                    router=(p['moe_w_router'][j], p['moe_b_router'][j]),
                                    name="norm2_router_p")
            n_experts = p['moe_w_router'].shape[-1]
            pos, row_token, tile_expert, tile_active, tile_first = _moe_dispatch(
                idx[:, :2], n_experts, MOE_TILE)
            take = lambda a, rows: a.at[rows].get(mode="promise_in_bounds")
            xg = take(h2, row_token)
            yg = moe_grouped(xg, tile_expert, tile_active, tile_first, p['moe_w_gate'][j],
                             p['moe_w_up'][j], p['moe_w_down'][j], tg=MOE_TILE)
            x = moe_combine(x, g2, take(yg, pos[:, 0]), take(yg, pos[:, 1]), wts, T)
        news.append((sbk_t.reshape(B, 8, 64, T).transpose(0, 3, 1, 2),
                     sbv_t.reshape(B, 8, 64, T).transpose(0, 3, 1, 2),
                     dfk_t.reshape(B, 4, 2, 64, T).transpose(0, 4, 1, 2, 3),
                     z3[5].reshape(B, T, 4, 128), conv_new, h_new, c_new, n_new, m_new))
    y = norm_mod(x, p['g_final'], out_dtype=F32, name="final_norm_p").reshape(B, T, D)
    return y, tuple(jnp.stack([nw[i] for nw in news], axis=0) for i in range(9))


def _sample_trunk(x3, mods, p, caches, states, page_table):
    G, _, D = x3.shape
    x = x3.reshape(G, D)
    sb_k_c, sb_v_c, df_k_c, df_v_c = caches
    n_layers, n_phys, page = sb_k_c.shape[:3]
    n_pool = n_layers * n_phys
    pool_sb_k = sb_k_c.transpose(0, 1, 3, 4, 2).reshape(n_pool, 8, 64, page)
    pool_sb_v = sb_v_c.transpose(0, 1, 3, 4, 2).reshape(n_pool, 8, 64, page)
    pool_df_k = df_k_c.transpose(0, 1, 3, 4, 5, 2).reshape(n_pool, 8, 64, page)
    pool_df_v = df_v_c.reshape(n_pool, page * 4, LANE)
    n_pages = page_table.shape[1]
    conv_all, h_all, c_all, n_all, m_all = states
    news = []
    for l in range(len(mods)):
        sh1, sc1, g1, sh2, sc2, g2 = mods[l]
        rows = (page_table + l * n_phys).reshape(-1).astype(jnp.int32)
        h = norm_mod(x, p['g_norm1'][l], sc1, sh1, out_dtype=F32, name="norm1_s")
        z = matmul(h, p['w_in'].swapaxes(1, 2), precise=True, layer=l, n_cols=N_MAIN, tn=W_GROUP,
                   b_transposed=True, name="w_in_s")
        gates = matmul(h, _gate_weight(p['w_in'], l), precise=True, name="gates_s")
        zs = [z[:, i * W_GROUP:(i + 1) * W_GROUP] for i in range(12)]
        sb_q, sb_k, sb_v, df_q, df_k, df_v, lru_x, lru_g, ml_q, ml_k, ml_v, ml_o = zs
        ml_i = gates[:, 0:4] + p['ml_b_i'][l][None, :]
        ml_f = gates[:, 4:8] + p['ml_b_f'][l][None, :]
        Tp = n_pages * page
        z_sb = decode_scores(sb_q, pool_sb_k, rows, n_pages).reshape(G, 8, Tp)
        w_sb = sb_weights(z_sb).reshape(G, 8, 1, Tp)
        y_sb = decode_pv_t(w_sb, pool_sb_v, rows, n_pages)
        y_sb = group_norm(y_sb, p['g_sb_out'][l], group=64)
        z_df = decode_scores(df_q, pool_df_k, rows, n_pages)
        z_df = z_df.reshape(G, 4, 2, Tp).swapaxes(1, 2).reshape(G, 8, Tp)
        w_df, new_df = diff_weights(z_df, df_q, df_k, df_v, p['diff_lam'][l], _lam_init(l))
        y_df = decode_pv_rows(w_df, pool_df_v, rows, new_df, n_pages)
        y_df = group_norm(y_df, p['g_diff_out'][l], group=128, post_scale=1.0 - _lam_init(l))
        y_lru, conv_new, h_new = lru_step(lru_x, lru_g, conv_all[l], h_all[l], p, l)
        y_ml, c_new, n_new, m_new = mlstm_step(ml_q, ml_k, ml_v, ml_o, ml_i, ml_f, c_all[l],
                                               n_all[l], m_all[l], p['g_ml_out'][l])
        ycat = jnp.concatenate([y_sb, y_df, y_lru, y_ml], axis=-1)
        x = matmul(ycat, p['w_out'], precise=True, layer=l, resid=(x, g1), name="w_out_s")
        j = l // 2
        if l % 2 == 0:
            h2 = norm_mod(x, p['g_norm2'][l], sc2, sh2, out_dtype=F32, name="norm2_s")
            act = swiglu_up(h2, p['ffn_w_gate'], p['ffn_w_up'], precise=True, out_dtype=F32,
                            layer=j, tk=1024, name="ffn_up_s")
            x = matmul(act, p['ffn_w_down'], precise=True, layer=j, tk=1408, resid=(x, g2),
                       name="ffn_down_s")
        else:
            h2, idx, wts = norm_mod(x, p['g_norm2'][l], sc2, sh2, out_dtype=F32,
                                    router=(p['moe_w_router'][j], p['moe_b_router'][j]),
                                    name="norm2_router_s")
            n_experts = p['moe_w_router'].shape[-1]
            e_ids = jnp.arange(n_experts, dtype=jnp.int32)[:, None]
            combine = (jnp.where(idx[None, :, 0] == e_ids, wts[None, :, 0], 0.0)
                       + jnp.where(idx[None, :, 1] == e_ids, wts[None, :, 1], 0.0))[..., None]
            x = moe_dense(h2, x, g2, combine, p['moe_w_gate'][j], p['moe_w_up'][j], p['moe_w_down'][j])
        news.append((sb_k.reshape(G, 1, 8, 64), sb_v.reshape(G, 1, 8, 64),
                     df_k.reshape(G, 1, 4, 2, 64), df_v.reshape(G, 1, 4, 128),
                     conv_new, h_new, c_new, n_new, m_new))
    y = norm_mod(x, p['g_final'], out_dtype=F32, name="final_norm_s").reshape(G, 1, D)
    return y, tuple(jnp.stack([nw[i] for nw in news], axis=0) for i in range(9))


def kernel(x_prompt, x_sample, cache_sb_k, cache_sb_v, cache_diff_k, cache_diff_v,
           state_lru_conv, state_lru_h, state_mlstm_c, state_mlstm_n, state_mlstm_m,
           page_table, c_prompt, c_sample, w_ada, b_ada, g_norm1, g_norm2, w_in, w_out,
           g_sb_out, diff_lam, g_diff_out, lru_conv_w, lru_conv_b, lru_w_r, lru_b_r,
           lru_w_i, lru_b_i, lru_lam, g_lru_out, ml_b_i, ml_b_f, g_ml_out,
           ffn_w_gate, ffn_w_up, ffn_w_down, moe_w_router, moe_b_router,
           moe_w_gate, moe_w_up, moe_w_down, g_final):
    p = dict(w_ada=w_ada, b_ada=b_ada, g_norm1=g_norm1, g_norm2=g_norm2, w_in=w_in, w_out=w_out,
             g_sb_out=g_sb_out, diff_lam=diff_lam, g_diff_out=g_diff_out, lru_conv_w=lru_conv_w,
             lru_conv_b=lru_conv_b, lru_w_r=lru_w_r, lru_b_r=lru_b_r, lru_w_i=lru_w_i,
             lru_b_i=lru_b_i, lru_lam=lru_lam, g_lru_out=g_lru_out, ml_b_i=ml_b_i, ml_b_f=ml_b_f,
             g_ml_out=g_ml_out, ffn_w_gate=ffn_w_gate, ffn_w_up=ffn_w_up, ffn_w_down=ffn_w_down,
             moe_w_router=moe_w_router, moe_b_router=moe_b_router, moe_w_gate=moe_w_gate,
             moe_w_up=moe_w_up, moe_w_down=moe_w_down, g_final=g_final)
    depth, D = g_norm1.shape
    Bp, Gs = x_prompt.shape[0], x_sample.shape[0]
    n_c = Bp + Gs
    c_all = jnp.zeros(((n_c + 15) // 16 * 16, D), F32).at[:Bp].set(c_prompt).at[Bp:n_c].set(c_sample)
    mods_p, mods_s = [], []
    for l in range(depth):
        m = matmul(c_all, w_ada, precise=True, layer=l, bias=b_ada[l], silu_a=True, name="ada")
        mods_p.append([m[:Bp, i * D:(i + 1) * D] for i in range(6)])
        mods_s.append([m[Bp:n_c, i * D:(i + 1) * D] for i in range(6)])
    wb = {k: p[k].astype(BF16) for k in ('w_in', 'w_out', 'ffn_w_gate', 'ffn_w_up', 'ffn_w_down')}
    y_p, new_p = _prompt_trunk(x_prompt, mods_p, p, wb)
    y_s, new_s = _sample_trunk(x_sample, mods_s, p,
                               (cache_sb_k, cache_sb_v, cache_diff_k, cache_diff_v),
                               (state_lru_conv, state_lru_h, state_mlstm_c, state_mlstm_n,
                                state_mlstm_m), page_table)
    return (y_p, y_s) + new_p + new_s
```
